```python
import jax, jax.numpy as jnp
from jax import lax
import numpy as np

D_MODEL = 1024
BATCH = 32
SEQ = 2048
DEPTH = 1
DEC_BATCH = 8
DEC_SEQ = 32
PAST_LEN = 1024

CHUNK = 64
C_CONV = D_MODEL // 2
CONV_WIDTH = 31
N_HEADS = 8
HEAD_DIM = 64
D_ATTN = N_HEADS * HEAD_DIM
D_FF = 4 * D_MODEL
Q_BLOCK = 128
LN_EPS = 1e-5
ATTN_SCALE = HEAD_DIM ** -0.5
NEG_INF = -1e30
DN_ALPHA = (2 * DEPTH) ** 0.25
DN_BETA = (8 * DEPTH) ** -0.25

OFF_GLU = 0
OFF_Q = OFF_GLU + 2 * C_CONV
OFF_K = OFF_Q + D_ATTN
OFF_V = OFF_K + D_ATTN
OFF_F = OFF_V + D_ATTN
OFF_G = OFF_F + N_HEADS
D_IN = OFF_G + 2 * D_MODEL

kernel_name = "conformer_conv_fox_gated_hybrid_step"


def _layer_norm(x, g, b):
    xf = x.astype(jnp.float32)
    mu = jnp.mean(xf, axis=-1, keepdims=True)
    var = jnp.mean(jnp.square(xf - mu), axis=-1, keepdims=True)
    y = (xf - mu) * lax.rsqrt(var + LN_EPS) * g.astype(jnp.float32) + b.astype(jnp.float32)
    return y.astype(x.dtype)


def _in_proj(x, w_in, b_f):
    bsz, t, _ = x.shape
    p = jnp.einsum('btd,de->bte', x, w_in)
    u = p[..., OFF_GLU:OFF_GLU + C_CONV] * jax.nn.sigmoid(p[..., OFF_GLU + C_CONV:OFF_Q])
    q = p[..., OFF_Q:OFF_K].reshape(bsz, t, N_HEADS, HEAD_DIM)
    k = p[..., OFF_K:OFF_V].reshape(bsz, t, N_HEADS, HEAD_DIM)
    v = p[..., OFF_V:OFF_F].reshape(bsz, t, N_HEADS, HEAD_DIM)
    logf = jax.nn.log_sigmoid(p[..., OFF_F:OFF_G].astype(jnp.float32) + b_f.astype(jnp.float32))
    g_conv = jax.nn.sigmoid(p[..., OFF_G:OFF_G + D_MODEL])
    g_attn = jax.nn.sigmoid(p[..., OFF_G + D_MODEL:D_IN])
    return u, q, k, v, logf, g_conv, g_attn


def _conv_branch(u_ext, w_dw, b_dw, ln_g, ln_b, w_out):
    y = lax.conv_general_dilated(
        u_ext, w_dw[:, None, :].astype(u_ext.dtype), window_strides=(1,), padding='VALID',
        dimension_numbers=('NWC', 'WIO', 'NWC'), feature_group_count=C_CONV)
    y = jax.nn.silu(_layer_norm(y + b_dw, ln_g, ln_b))
    return jnp.einsum('btc,cd->btd', y, w_out)


def _fox_attend(q, k, v, cq, ck, q_pos, k_pos):
    s = jnp.einsum('bqhd,bkhd->bhqk', q, k).astype(jnp.float32) * ATTN_SCALE
    bias = jnp.swapaxes(cq, 1, 2)[:, :, :, None] - jnp.swapaxes(ck, 1, 2)[:, :, None, :]
    mask = k_pos[None, :] <= q_pos[:, None]
    s = jnp.where(mask, s + bias, NEG_INF)
    p = jax.nn.softmax(s, axis=-1)
    return jnp.einsum('bhqk,bkhd->bqhd', p.astype(v.dtype), v)


def _fox_prompt(q, k, v, logf):
    bsz, t = q.shape[:2]
    c = jnp.cumsum(logf, axis=1)
    k_pos = jnp.arange(t)

    def block(i):
        start = i * Q_BLOCK
        qb = lax.dynamic_slice_in_dim(q, start, Q_BLOCK, axis=1)
        cb = lax.dynamic_slice_in_dim(c, start, Q_BLOCK, axis=1)
        return _fox_attend(qb, k, v, cb, c, start + jnp.arange(Q_BLOCK), k_pos)

    o = lax.map(block, jnp.arange(t // Q_BLOCK))
    return jnp.moveaxis(o, 0, 1).reshape(bsz, t, D_ATTN)


def _fox_sample(q, k, v, logf, cache_k, cache_v, cache_logf):
    bsz, t = q.shape[:2]
    past = cache_k.shape[1]
    k_all = jnp.concatenate([cache_k, k], axis=1)
    v_all = jnp.concatenate([cache_v, v], axis=1)
    c = jnp.cumsum(jnp.concatenate([cache_logf.astype(jnp.float32), logf], axis=1), axis=1)
    o = _fox_attend(q, k_all, v_all, c[:, past:], c, past + jnp.arange(t), jnp.arange(past + t))
    return o.reshape(bsz, t, D_ATTN)


def _tail(x, y_conv, o_attn, g_conv, g_attn, w_attn_out, w_o, ln1_g, ln1_b, w_up, w_down, ln2_g, ln2_b):
    y_attn = jnp.einsum('bte,ed->btd', o_attn, w_attn_out)
    mix = jnp.einsum('btd,de->bte', g_conv * y_conv + g_attn * y_attn, w_o)
    x1 = _layer_norm(DN_ALPHA * x + mix, ln1_g, ln1_b)
    h = jnp.square(jax.nn.relu(jnp.einsum('btd,df->btf', x1, w_up)))
    return _layer_norm(DN_ALPHA * x1 + jnp.einsum('btf,fd->btd', h, w_down), ln2_g, ln2_b)


def setup_inputs(seed: int = 0) -> dict:
    key = jax.random.key(seed)
    ks = jax.random.split(key, 24)
    f32 = jnp.float32

    def nrm(k, shape, scale):
        return jax.random.normal(k, shape, f32) * scale

    return {
        "x_prompt": nrm(ks[0], (BATCH, SEQ, D_MODEL), 1.0),
        "x_sample": nrm(ks[1], (DEC_BATCH, DEC_SEQ, D_MODEL), 1.0),
        "cache_conv": nrm(ks[2], (DEPTH, DEC_BATCH, CONV_WIDTH - 1, C_CONV), 0.5),
        "cache_k": nrm(ks[3], (DEPTH, DEC_BATCH, PAST_LEN, N_HEADS, HEAD_DIM), 1.0),
        "cache_v": nrm(ks[4], (DEPTH, DEC_BATCH, PAST_LEN, N_HEADS, HEAD_DIM), 1.0),
        "cache_logf": jax.nn.log_sigmoid(3.0 + nrm(ks[5], (DEPTH, DEC_BATCH, PAST_LEN, N_HEADS), 1.0)),
        "w_in": nrm(ks[6], (DEPTH, D_MODEL, D_IN), D_MODEL ** -0.5),
        "b_f": 3.0 + nrm(ks[7], (DEPTH, N_HEADS), 0.5),
        "w_dw": nrm(ks[8], (DEPTH, CONV_WIDTH, C_CONV), CONV_WIDTH ** -0.5),
        "b_dw": nrm(ks[9], (DEPTH, C_CONV), 0.02),
        "ln_conv_g": 1.0 + nrm(ks[10], (DEPTH, C_CONV), 0.02),
        "ln_conv_b": nrm(ks[11], (DEPTH, C_CONV), 0.02),
        "w_conv_out": nrm(ks[12], (DEPTH, C_CONV, D_MODEL), C_CONV ** -0.5),
        "w_attn_out": nrm(ks[13], (DEPTH, D_ATTN, D_MODEL), D_ATTN ** -0.5),
        "w_o": nrm(ks[14], (DEPTH, D_MODEL, D_MODEL), DN_BETA * D_MODEL ** -0.5),
        "ln1_g": 1.0 + nrm(ks[15], (DEPTH, D_MODEL), 0.02),
        "ln1_b": nrm(ks[16], (DEPTH, D_MODEL), 0.02),
        "w_up": nrm(ks[17], (DEPTH, D_MODEL, D_FF), D_MODEL ** -0.5),
        "w_down": nrm(ks[18], (DEPTH, D_FF, D_MODEL), DN_BETA * D_FF ** -0.5),
        "ln2_g": 1.0 + nrm(ks[19], (DEPTH, D_MODEL), 0.02),
        "ln2_b": nrm(ks[20], (DEPTH, D_MODEL), 0.02),
    }


def reference(x_prompt, x_sample, cache_conv, cache_k, cache_v, cache_logf,
              w_in, b_f, w_dw, b_dw, ln_conv_g, ln_conv_b, w_conv_out, w_attn_out, w_o,
              ln1_g, ln1_b, w_up, w_down, ln2_g, ln2_b):
    xp, xs = x_prompt, x_sample
    conv_p, k_p, v_p, lf_p = [], [], [], []
    conv_s, k_s, v_s, lf_s = [], [], [], []
    hist = CONV_WIDTH - 1
    for l in range(DEPTH):
        u, q, k, v, logf, gc, ga = _in_proj(xp, w_in[l], b_f[l])
        u_ext = jnp.pad(u, ((0, 0), (hist, 0), (0, 0)))
        y_conv = _conv_branch(u_ext, w_dw[l], b_dw[l], ln_conv_g[l], ln_conv_b[l], w_conv_out[l])
        o_attn = _fox_prompt(q, k, v, logf)
        xp = _tail(xp, y_conv, o_attn, gc, ga, w_attn_out[l], w_o[l],
                   ln1_g[l], ln1_b[l], w_up[l], w_down[l], ln2_g[l], ln2_b[l])
        conv_p.append(u_ext[:, -hist:])
        k_p.append(k)
        v_p.append(v)
        lf_p.append(logf)
        u, q, k, v, logf, gc, ga = _in_proj(xs, w_in[l], b_f[l])
        u_ext = jnp.concatenate([cache_conv[l].astype(u.dtype), u], axis=1)
        y_conv = _conv_branch(u_ext, w_dw[l], b_dw[l], ln_conv_g[l], ln_conv_b[l], w_conv_out[l])
        o_attn = _fox_sample(q, k, v, logf, cache_k[l], cache_v[l], cache_logf[l])
        xs = _tail(xs, y_conv, o_attn, gc, ga, w_attn_out[l], w_o[l],
                   ln1_g[l], ln1_b[l], w_up[l], w_down[l], ln2_g[l], ln2_b[l])
        conv_s.append(u_ext[:, -hist:])
        k_s.append(k)
        v_s.append(v)
        lf_s.append(logf)
    return (xp, xs,
            jnp.stack(conv_p), jnp.stack(k_p), jnp.stack(v_p), jnp.stack(lf_p),
            jnp.stack(conv_s), jnp.stack(k_s), jnp.stack(v_s), jnp.stack(lf_s))
```

```python
import functools

import jax
import jax.numpy as jnp
from jax import lax
from jax.experimental import pallas as pl
from jax.experimental.pallas import tpu as pltpu

D_MODEL = 1024
C_CONV = D_MODEL // 2
CONV_WIDTH = 31
HIST = CONV_WIDTH - 1
HIST_PAD = 32
N_HEADS = 8
HEAD_DIM = 64
D_ATTN = N_HEADS * HEAD_DIM
D_FF = 4 * D_MODEL
LN_EPS = 1e-5
ATTN_SCALE = HEAD_DIM ** -0.5
NEG_INF = -1e30
DN_ALPHA = 2.0 ** 0.25

OFF_Q = 2 * C_CONV
OFF_K = OFF_Q + D_ATTN
OFF_V = OFF_K + D_ATTN
OFF_F = OFF_V + D_ATTN
OFF_G = OFF_F + N_HEADS

LANES = 128
F_PAD = LANES
FF_CHUNK = 1024
VMEM_LIMIT = 58 * 1024 * 1024

F32 = jnp.float32
BF16 = jnp.bfloat16


def _layer_norm(x, g, b):
    mu = jnp.mean(x, axis=-1, keepdims=True)
    xc = x - mu
    var = jnp.mean(xc * xc, axis=-1, keepdims=True)
    return xc * lax.rsqrt(var + LN_EPS) * g + b


def _sigmoid(x):
    return 1.0 / (1.0 + jnp.exp(-x))


def _proj_kernel(x_ref, w_ref, bf_ref, q_ref, k_ref, v_ref, kt_ref, vb_ref, lf_ref):
    x = x_ref[0].astype(BF16)
    p = jnp.dot(x, w_ref[...], preferred_element_type=F32)
    q_ref[0] = p[:, 0:D_ATTN].astype(BF16)
    k = p[:, D_ATTN:2 * D_ATTN]
    v = p[:, 2 * D_ATTN:3 * D_ATTN]
    k_ref[0] = k
    v_ref[0] = v
    kt_ref[0] = k.T.astype(BF16)
    vb_ref[0] = v.astype(BF16)
    z = p[:, 3 * D_ATTN:3 * D_ATTN + N_HEADS] + bf_ref[...]
    lf_ref[0] = jnp.minimum(z, 0.0) - jnp.log1p(jnp.exp(-jnp.abs(z)))


def _proj(x, w_qkvf, b_f, tm):
    bsz, t, d = x.shape
    grid = (bsz, t // tm)
    n_out = w_qkvf.shape[1]
    tile = lambda b, i: (b, i, 0)
    return pl.pallas_call(
        _proj_kernel,
        grid=grid,
        in_specs=[
            pl.BlockSpec((1, tm, d), tile),
            pl.BlockSpec((d, n_out), lambda b, i: (0, 0)),
            pl.BlockSpec((1, N_HEADS), lambda b, i: (0, 0)),
        ],
        out_specs=[
            pl.BlockSpec((1, tm, D_ATTN), tile),
            pl.BlockSpec((1, tm, D_ATTN), tile),
            pl.BlockSpec((1, tm, D_ATTN), tile),
            pl.BlockSpec((1, D_ATTN, tm), lambda b, i: (b, 0, i)),
            pl.BlockSpec((1, tm, D_ATTN), tile),
            pl.BlockSpec((1, tm, N_HEADS), tile),
        ],
        out_shape=[
            jax.ShapeDtypeStruct((bsz, t, D_ATTN), BF16),
            jax.ShapeDtypeStruct((bsz, t, D_ATTN), F32),
            jax.ShapeDtypeStruct((bsz, t, D_ATTN), F32),
            jax.ShapeDtypeStruct((bsz, D_ATTN, t), BF16),
            jax.ShapeDtypeStruct((bsz, t, D_ATTN), BF16),
            jax.ShapeDtypeStruct((bsz, t, N_HEADS), F32),
        ],
        compiler_params=pltpu.CompilerParams(
            dimension_semantics=("arbitrary", "arbitrary"),
            vmem_limit_bytes=VMEM_LIMIT),
        name="proj",
    )(x, w_qkvf, b_f)


def _attn_kernel(q_ref, kt_ref, v_ref, lft_ref, o_ref,
                 c_scr, bias_scr, s_scr, m_scr, l_scr, acc_scr,
                 *, tq, tk, past, length):
    i = pl.program_id(1)
    groups = tk // LANES

    @pl.when(i == 0)
    def _():
        c = lft_ref[0]
        lane = lax.broadcasted_iota(jnp.int32, c.shape, 1)
        shift = 1
        while shift < length:
            c = c + jnp.where(lane >= shift, pltpu.roll(c, shift, 1), 0.0)
            shift *= 2
        c_scr[...] = c

    q0 = past + i * tq
    n_full = q0 // tk
    offset = q0 - n_full * tk

    c_all = c_scr[...]
    pos = lax.broadcasted_iota(jnp.int32, c_all.shape, 1)
    in_tile = (pos >= q0) & (pos < q0 + tq)
    c_min = jnp.min(jnp.where(in_tile, c_all, jnp.inf), axis=1, keepdims=True)
    bias_scr[...] = c_min - c_all

    q = q_ref[0]
    m_scr[...] = jnp.full(m_scr.shape, NEG_INF, F32)

    def scores(j, masked):
        k0 = pl.multiple_of(j * tk, tk)
        for h in range(N_HEADS):
            hs = slice(h * HEAD_DIM, (h + 1) * HEAD_DIM)
            s = jnp.dot(q[:, hs], kt_ref[0, hs, pl.ds(k0, tk)],
                        preferred_element_type=F32)
            s = s + bias_scr[h:h + 1, pl.ds(k0, tk)]
            if masked:
                row = lax.broadcasted_iota(jnp.int32, (tq, tk), 0)
                col = lax.broadcasted_iota(jnp.int32, (tq, tk), 1)
                s = jnp.where(col <= row + offset, s, NEG_INF)
            s_scr[h, :, pl.ds(k0, tk)] = s
            m = m_scr[h]
            for g in range(groups):
                m = jnp.maximum(m, s[:, g * LANES:(g + 1) * LANES])
            m_scr[h] = m

    def pass1(j, carry):
        scores(j, masked=False)
        return carry

    lax.fori_loop(0, n_full, pass1, 0)
    scores(n_full, masked=True)

    for h in range(N_HEADS):
        m = jnp.max(m_scr[h], axis=1, keepdims=True)
        m_scr[h] = jnp.broadcast_to(m, (tq, LANES))
    l_scr[...] = jnp.zeros(l_scr.shape, F32)
    acc_scr[...] = jnp.zeros(acc_scr.shape, F32)

    def pass2(j, carry):
        k0 = pl.multiple_of(j * tk, tk)
        for h in range(N_HEADS):
            hs = slice(h * HEAD_DIM, (h + 1) * HEAD_DIM)
            m = m_scr[h]
            l = l_scr[h]
            ps = []
            for g in range(groups):
                p = jnp.exp(s_scr[h, :, pl.ds(k0 + g * LANES, LANES)] - m)
                l = l + p
                ps.append(p.astype(BF16))
            l_scr[h] = l
            p = jnp.concatenate(ps, axis=1) if groups > 1 else ps[0]
            acc_scr[h] += jnp.dot(p, v_ref[0, pl.ds(k0, tk), hs],
                                  preferred_element_type=F32)
        return carry

    lax.fori_loop(0, n_full + 1, pass2, 0)

    for h in range(N_HEADS):
        l = jnp.sum(l_scr[h], axis=1, keepdims=True)
        o_ref[0, :, h * HEAD_DIM:(h + 1) * HEAD_DIM] = (acc_scr[h] / l).astype(BF16)


def _attn(q, kt, v, lft, tq, tk, past):
    bsz, t_q, _ = q.shape
    length = kt.shape[2]
    assert tk % LANES == 0 and tk % tq == 0 and past % tq == 0 and length % tk == 0
    assert past + t_q <= length
    kern = functools.partial(_attn_kernel, tq=tq, tk=tk, past=past, length=length)
    return pl.pallas_call(
        kern,
        grid=(bsz, t_q // tq),
        in_specs=[
            pl.BlockSpec((1, tq, D_ATTN), lambda b, i: (b, i, 0)),
            pl.BlockSpec((1, D_ATTN, length), lambda b, i: (b, 0, 0)),
            pl.BlockSpec((1, length, D_ATTN), lambda b, i: (b, 0, 0)),
            pl.BlockSpec((1, N_HEADS, length), lambda b, i: (b, 0, 0)),
        ],
        out_specs=pl.BlockSpec((1, tq, D_ATTN), lambda b, i: (b, i, 0)),
        out_shape=jax.ShapeDtypeStruct((bsz, t_q, D_ATTN), BF16),
        scratch_shapes=[
            pltpu.VMEM((N_HEADS, length), F32),
            pltpu.VMEM((N_HEADS, length), F32),
            pltpu.VMEM((N_HEADS, tq, length), F32),
            pltpu.VMEM((N_HEADS, tq, LANES), F32),
            pltpu.VMEM((N_HEADS, tq, LANES), F32),
            pltpu.VMEM((N_HEADS, tq, HEAD_DIM), F32),
        ],
        compiler_params=pltpu.CompilerParams(
            dimension_semantics=("arbitrary", "arbitrary"),
            vmem_limit_bytes=VMEM_LIMIT),
        name="attn",
    )(q, kt, v, lft)


def _tail_kernel(x_ref, o_ref, hist_ref,
                 w_glu, w_gate, w_dw, b_dw, lnc_g, lnc_b, w_co, w_ao, w_o,
                 ln1_g, ln1_b, w_up, w_dn, ln2_g, ln2_b,
                 y_ref, cs_ref, u_scr, *, nb, tm):
    t = pl.program_id(1)
    rows = nb * tm

    @pl.when(t == 0)
    def _():
        u_scr[:, 0:HIST_PAD, :] = hist_ref[...]

    x = x_ref[...].reshape(rows, D_MODEL)
    xb = x.astype(BF16)

    g = jnp.dot(xb, w_glu[...], preferred_element_type=F32)
    u = g[:, :C_CONV] * _sigmoid(g[:, C_CONV:])
    u_scr[:, HIST_PAD:HIST_PAD + tm, :] = u.reshape(nb, tm, C_CONV)

    first = HIST_PAD - HIST
    convs = []
    for b in range(nb):
        acc = jnp.zeros((tm, C_CONV), F32)
        for j in range(CONV_WIDTH):
            acc = acc + w_dw[j:j + 1, :] * u_scr[b, first + j:first + j + tm, :]
        convs.append(acc)
    yc = jnp.concatenate(convs, axis=0) if nb > 1 else convs[0]
    u_scr[:, 0:HIST_PAD, :] = u_scr[:, tm:tm + HIST_PAD, :]
    cs_ref[...] = u_scr[:, 0:HIST_PAD, :]

    yc = _layer_norm(yc + b_dw[...], lnc_g[...], lnc_b[...])
    yc = yc * _sigmoid(yc)
    y_conv = jnp.dot(yc.astype(BF16), w_co[...], preferred_element_type=F32)

    o = o_ref[...].reshape(rows, D_ATTN)
    y_attn = jnp.dot(o, w_ao[...], preferred_element_type=F32)

    gates = _sigmoid(jnp.dot(xb, w_gate[...], preferred_element_type=F32))
    merged = gates[:, :D_MODEL] * y_conv + gates[:, D_MODEL:] * y_attn
    mix = jnp.dot(merged.astype(BF16), w_o[...], preferred_element_type=F32)

    x1 = _layer_norm(DN_ALPHA * x + mix, ln1_g[...], ln1_b[...])
    x1b = x1.astype(BF16)
    ff = jnp.zeros((rows, D_MODEL), F32)
    for c in range(D_FF // FF_CHUNK):
        cs = slice(c * FF_CHUNK, (c + 1) * FF_CHUNK)
        h = jnp.maximum(jnp.dot(x1b, w_up[:, cs], preferred_element_type=F32), 0.0)
        ff = ff + jnp.dot((h * h).astype(BF16), w_dn[cs, :], preferred_element_type=F32)
    x2 = _layer_norm(DN_ALPHA * x1 + ff, ln2_g[...], ln2_b[...])
    y_ref[...] = x2.reshape(nb, tm, D_MODEL)


def _tail(x, o, hist, weights, nb, tm):
    bsz, t, _ = x.shape
    kern = functools.partial(_tail_kernel, nb=nb, tm=tm)
    tile = lambda b, i: (b, i, 0)

    def resident(w):
        return pl.BlockSpec(w.shape, lambda b, i: (0,) * w.ndim,
                            pipeline_mode=pl.Buffered(1))

    return pl.pallas_call(
        kern,
        grid=(bsz // nb, t // tm),
        in_specs=[
            pl.BlockSpec((nb, tm, D_MODEL), tile),
            pl.BlockSpec((nb, tm, D_ATTN), tile),
            pl.BlockSpec((nb, HIST_PAD, C_CONV), lambda b, i: (b, 0, 0)),
        ] + [resident(w) for w in weights],
        out_specs=[
            pl.BlockSpec((nb, tm, D_MODEL), tile),
            pl.BlockSpec((nb, HIST_PAD, C_CONV), lambda b, i: (b, 0, 0)),
        ],
        out_shape=[
            jax.ShapeDtypeStruct((bsz, t, D_MODEL), F32),
            jax.ShapeDtypeStruct((bsz, HIST_PAD, C_CONV), F32),
        ],
        scratch_shapes=[pltpu.VMEM((nb, HIST_PAD + tm, C_CONV), F32)],
        compiler_params=pltpu.CompilerParams(
            dimension_semantics=("arbitrary", "arbitrary"),
            vmem_limit_bytes=VMEM_LIMIT),
        name="tail",
    )(x, o, hist, *weights)


def _group(x, hist, past_kt, past_v, past_lft, w_qkvf, b_f, tail_w, *, tm_proj, tq, tk, nb, tm_tail):
    bsz, t, _ = x.shape
    q, k, v, kt, vb, lf = _proj(x, w_qkvf, b_f, tm_proj)
    lft = jnp.swapaxes(lf, 1, 2)
    past = 0
    if past_kt is not None:
        past = past_kt.shape[2]
        length = -(-(past + t) // tk) * tk
        pad = length - past - t
        kt = jnp.pad(jnp.concatenate([past_kt, kt], axis=2), ((0, 0), (0, 0), (0, pad)))
        vb = jnp.pad(jnp.concatenate([past_v, vb], axis=1), ((0, 0), (0, pad), (0, 0)))
        lft = jnp.pad(jnp.concatenate([past_lft, lft], axis=2), ((0, 0), (0, 0), (0, pad)))
    o = _attn(q, kt, vb, lft, tq, tk, past)
    hist = jnp.pad(hist, ((0, 0), (HIST_PAD - HIST, 0), (0, 0)))
    y, cs = _tail(x, o, hist, tail_w, nb, tm_tail)
    hd = (bsz, t, N_HEADS, HEAD_DIM)
    return y, cs[None, :, HIST_PAD - HIST:, :], k.reshape(hd)[None], v.reshape(hd)[None], lf[None]


def kernel(x_prompt, x_sample, cache_conv, cache_k, cache_v, cache_logf, w_in, b_f, w_dw, b_dw,
           ln_conv_g, ln_conv_b, w_conv_out, w_attn_out, w_o, ln1_g, ln1_b, w_up, w_down,
           ln2_g, ln2_b):
    assert w_in.shape[0] == 1, "single layer"
    w = w_in[0]
    row = lambda a: a[0][None, :].astype(F32)
    w_qkvf = jnp.concatenate(
        [w[:, OFF_Q:OFF_K] * ATTN_SCALE, w[:, OFF_K:OFF_F],
         jnp.pad(w[:, OFF_F:OFF_G], ((0, 0), (0, F_PAD - N_HEADS)))], axis=1).astype(BF16)
    tail_w = (
        w[:, 0:OFF_Q].astype(BF16), w[:, OFF_G:].astype(BF16),
        w_dw[0].astype(F32), row(b_dw), row(ln_conv_g), row(ln_conv_b),
        w_conv_out[0].astype(BF16), w_attn_out[0].astype(BF16), w_o[0].astype(BF16),
        row(ln1_g), row(ln1_b), w_up[0].astype(BF16), w_down[0].astype(BF16),
        row(ln2_g), row(ln2_b))
    bf = row(b_f)

    bp, tp, _ = x_prompt.shape
    bs, ts, _ = x_sample.shape
    yp, cp, kp, vp, lp = _group(
        x_prompt, jnp.zeros((bp, HIST, C_CONV), F32), None, None, None, w_qkvf, bf, tail_w,
        tm_proj=512, tq=256, tk=256, nb=1, tm_tail=512)

    past = cache_k.shape[2]
    past_kt = jnp.swapaxes(cache_k[0].reshape(bs, past, D_ATTN), 1, 2).astype(BF16)
    past_v = cache_v[0].reshape(bs, past, D_ATTN).astype(BF16)
    past_lft = jnp.swapaxes(cache_logf[0].astype(F32), 1, 2)
    tk_s = -(-(past + ts) // LANES) * LANES
    ys, cs, ks, vs, ls = _group(
        x_sample, cache_conv[0].astype(F32), past_kt, past_v, past_lft, w_qkvf, bf, tail_w,
        tm_proj=ts, tq=ts, tk=tk_s, nb=bs, tm_tail=ts)
    return (yp, ys, cp, kp, vp, lp, cs, ks, vs, ls)
```

```python
import functools

import jax
import jax.numpy as jnp
from jax import lax
from jax.experimental import pallas as pl
from jax.experimental.pallas import tpu as pltpu

D_MODEL = 1024
C_CONV = D_MODEL // 2
CONV_WIDTH = 31
HIST = CONV_WIDTH - 1
HIST_PAD = 32
N_HEADS = 8
HEAD_DIM = 64
D_ATTN = N_HEADS * HEAD_DIM
HEAD_BLOCK = 128
D_BLOCKS = N_HEADS * HEAD_BLOCK
BIAS_ROWS = 3
BF16_ROWS = 16
LOG2E = 1.4426950408889634
D_FF = 4 * D_MODEL
LN_EPS = 1e-5
ATTN_SCALE = HEAD_DIM ** -0.5
NEG_INF = -1e30
DN_ALPHA = 2.0 ** 0.25

OFF_Q = 2 * C_CONV
OFF_K = OFF_Q + D_ATTN
OFF_V = OFF_K + D_ATTN
OFF_F = OFF_V + D_ATTN
OFF_G = OFF_F + N_HEADS

LANES = 128
SUBLANES = 8
CONV_ROWS = 64
CONV_COLS = 256
F_PAD = LANES
FF_CHUNK = 1024
VMEM_LIMIT = 58 * 1024 * 1024

F32 = jnp.float32
BF16 = jnp.bfloat16


def _layer_norm(x, g, b):
    mu = jnp.mean(x, axis=-1, keepdims=True)
    xc = x - mu
    var = jnp.mean(xc * xc, axis=-1, keepdims=True)
    return xc * lax.rsqrt(var + LN_EPS) * g + b


def _sigmoid(x):
    return 1.0 / (1.0 + jnp.exp(-x))


def _head_blocks(x, n_ones):
    tm = x.shape[0]
    lane = lax.broadcasted_iota(jnp.int32, (tm, HEAD_BLOCK - HEAD_DIM), 1)
    tail = jnp.where(lane < n_ones, 1.0, 0.0).astype(F32)
    parts = []
    for h in range(N_HEADS):
        parts += [x[:, h * HEAD_DIM:(h + 1) * HEAD_DIM], tail]
    return jnp.concatenate(parts, axis=1).astype(BF16)


def _proj_kernel(x_ref, w_ref, bf_ref, q_ref, k_ref, v_ref, kt_ref, vb_ref, lf_ref):
    x = x_ref[0].astype(BF16)
    p = jnp.dot(x, w_ref[...], preferred_element_type=F32)
    q_ref[0] = _head_blocks(p[:, 0:D_ATTN], BIAS_ROWS)
    k = p[:, D_ATTN:2 * D_ATTN]
    v = p[:, 2 * D_ATTN:3 * D_ATTN]
    k_ref[0] = k
    v_ref[0] = v
    kt_ref[0] = k.T.astype(BF16)
    vb_ref[0] = _head_blocks(v, 1)
    z = p[:, 3 * D_ATTN:3 * D_ATTN + N_HEADS] + bf_ref[...]
    lf_ref[0] = jnp.minimum(z, 0.0) - jnp.log1p(jnp.exp(-jnp.abs(z)))


def _proj(x, w_qkvf, b_f, tm):
    bsz, t, d = x.shape
    grid = (bsz, t // tm)
    n_out = w_qkvf.shape[1]
    tile = lambda b, i: (b, i, 0)
    return pl.pallas_call(
        _proj_kernel,
        grid=grid,
        in_specs=[
            pl.BlockSpec((1, tm, d), tile),
            pl.BlockSpec((d, n_out), lambda b, i: (0, 0)),
            pl.BlockSpec((1, N_HEADS), lambda b, i: (0, 0)),
        ],
        out_specs=[
            pl.BlockSpec((1, tm, D_BLOCKS), tile),
            pl.BlockSpec((1, tm, D_ATTN), tile),
            pl.BlockSpec((1, tm, D_ATTN), tile),
            pl.BlockSpec((1, D_ATTN, tm), lambda b, i: (b, 0, i)),
            pl.BlockSpec((1, tm, D_BLOCKS), tile),
            pl.BlockSpec((1, tm, N_HEADS), tile),
        ],
        out_shape=[
            jax.ShapeDtypeStruct((bsz, t, D_BLOCKS), BF16),
            jax.ShapeDtypeStruct((bsz, t, D_ATTN), F32),
            jax.ShapeDtypeStruct((bsz, t, D_ATTN), F32),
            jax.ShapeDtypeStruct((bsz, D_ATTN, t), BF16),
            jax.ShapeDtypeStruct((bsz, t, D_BLOCKS), BF16),
            jax.ShapeDtypeStruct((bsz, t, N_HEADS), F32),
        ],
        compiler_params=pltpu.CompilerParams(
            dimension_semantics=("arbitrary", "arbitrary"),
            vmem_limit_bytes=VMEM_LIMIT),
        name="proj",
    )(x, w_qkvf, b_f)


def _attn_kernel(q_ref, kt_ref, v_ref, lft_ref, o_ref, kta_scr, s_scr, m_scr, acc_scr,
                 *, tq, tk, past, length):
    i = pl.program_id(1)

    @pl.when(i == 0)
    def _():
        c = lft_ref[0]
        lane = lax.broadcasted_iota(jnp.int32, c.shape, 1)
        shift = 1
        while shift < length:
            c = c + jnp.where(lane >= shift, pltpu.roll(c, shift, 1), 0.0)
            shift *= 2
        cl = c * (-LOG2E)
        hi = cl.astype(BF16).astype(F32)
        mid = (cl - hi).astype(BF16).astype(F32)
        lo = (cl - hi - mid).astype(BF16).astype(F32)
        row = lax.broadcasted_iota(jnp.int32, (BF16_ROWS, length), 0)
        for h in range(N_HEADS):
            base = h * HEAD_BLOCK
            kta_scr[base:base + HEAD_DIM, :] = kt_ref[0, h * HEAD_DIM:(h + 1) * HEAD_DIM, :]
            blk = jnp.where(row == 0, hi[h:h + 1], jnp.where(row == 1, mid[h:h + 1],
                            jnp.where(row == 2, lo[h:h + 1], 0.0)))
            kta_scr[base + HEAD_DIM:base + HEAD_DIM + BF16_ROWS, :] = blk.astype(BF16)
            kta_scr[base + HEAD_DIM + BF16_ROWS:base + HEAD_BLOCK, :] = jnp.zeros(
                (HEAD_BLOCK - HEAD_DIM - BF16_ROWS, length), BF16)

    n_full = (past + i * tq) // tk
    offset = past % tk
    if offset == 0 and tq == tk and tq % (2 * LANES) == 0:
        half = tq // 2
        diag_parts = [(0, half, half, 0), (half, tq, tk, half)]
    else:
        diag_parts = [(0, tq, tk, offset)]

    m_scr[...] = jnp.full(m_scr.shape, NEG_INF, F32)
    acc_scr[...] = jnp.zeros(acc_scr.shape, F32)

    def chunk(j, r0, r1, width, mask_offset):
        k0 = pl.multiple_of(j * tk, tk)
        for h in range(N_HEADS):
            hb = slice(h * HEAD_BLOCK, (h + 1) * HEAD_BLOCK)
            s = jnp.dot(q_ref[0, r0:r1, hb], kta_scr[hb, pl.ds(k0, width)],
                        preferred_element_type=F32)
            if mask_offset is not None:
                row = lax.broadcasted_iota(jnp.int32, s.shape, 0)
                col = lax.broadcasted_iota(jnp.int32, s.shape, 1)
                s = jnp.where(col <= row + mask_offset, s, NEG_INF)
            s_scr[h, r0:r1, 0:width] = s
            m_cur = s[:, 0:LANES]
            for g in range(1, width // LANES):
                m_cur = jnp.maximum(m_cur, s[:, g * LANES:(g + 1) * LANES])
            m_old = m_scr[h, r0:r1]
            m_new = jnp.maximum(m_old, jnp.max(m_cur, axis=1, keepdims=True))
            m_scr[h, r0:r1] = m_new
            alpha = jnp.exp2(m_old - m_new)
            p = jnp.concatenate(
                [jnp.exp2(s_scr[h, r0:r1, g * LANES:(g + 1) * LANES] - m_new).astype(BF16)
                 for g in range(width // LANES)], axis=1)
            pv = jnp.dot(p, v_ref[0, pl.ds(k0, width), hb], preferred_element_type=F32)
            acc_scr[h, r0:r1] = alpha * acc_scr[h, r0:r1] + pv

    def full_chunk(j, carry):
        chunk(j, 0, tq, tk, None)
        return carry

    lax.fori_loop(0, n_full, full_chunk, 0)
    for r0, r1, width, mask_offset in diag_parts:
        chunk(n_full, r0, r1, width, mask_offset)

    for h in range(0, N_HEADS, 2):
        pair = []
        for hh in (h, h + 1):
            a = acc_scr[hh]
            pair.append(a[:, 0:HEAD_DIM] / a[:, HEAD_DIM:HEAD_DIM + 1])
        o_ref[0, :, h * HEAD_DIM:(h + 2) * HEAD_DIM] = jnp.concatenate(pair, axis=1).astype(BF16)


def _attn(q, kt, v, lft, tq, tk, past):
    bsz, t_q, _ = q.shape
    length = kt.shape[2]
    assert tk % LANES == 0 and length % tk == 0 and past + t_q <= length
    assert (tq % tk == 0 or t_q == tq) and past % tk + tq <= tk
    kern = functools.partial(_attn_kernel, tq=tq, tk=tk, past=past, length=length)
    return pl.pallas_call(
        kern,
        grid=(bsz, t_q // tq),
        in_specs=[
            pl.BlockSpec((1, tq, D_BLOCKS), lambda b, i: (b, i, 0)),
            pl.BlockSpec((1, D_ATTN, length), lambda b, i: (b, 0, 0)),
            pl.BlockSpec((1, length, D_BLOCKS), lambda b, i: (b, 0, 0)),
            pl.BlockSpec((1, N_HEADS, length), lambda b, i: (b, 0, 0)),
        ],
        out_specs=pl.BlockSpec((1, tq, D_ATTN), lambda b, i: (b, i, 0)),
        out_shape=jax.ShapeDtypeStruct((bsz, t_q, D_ATTN), BF16),
        scratch_shapes=[
            pltpu.VMEM((D_BLOCKS, length), BF16),
            pltpu.VMEM((N_HEADS, tq, tk), F32),
            pltpu.VMEM((N_HEADS, tq, LANES), F32),
            pltpu.VMEM((N_HEADS, tq, HEAD_BLOCK), F32),
        ],
        compiler_params=pltpu.CompilerParams(
            dimension_semantics=("arbitrary", "arbitrary"),
            vmem_limit_bytes=VMEM_LIMIT),
        name="attn",
    )(q, kt, v, lft)


def _causal_conv(u_scr, b, w_dw, tm):
    first = HIST_PAD - HIST
    rb = min(tm, CONV_ROWS)
    row_blocks = []
    for r0 in range(0, tm, rb):
        col_blocks = []
        for c0 in range(0, C_CONV, CONV_COLS):
            cols = slice(c0, c0 + CONV_COLS)
            y = None
            for r in range(SUBLANES):
                off = first + r
                base = off // SUBLANES * SUBLANES
                sh = off - base
                n = rb + (SUBLANES if sh else 0)
                z = None
                for j in range(r, CONV_WIDTH, SUBLANES):
                    lo = r0 + base + j - r
                    term = w_dw[j:j + 1, cols] * u_scr[b, lo:lo + n, cols]
                    z = term if z is None else z + term
                z = z[sh:sh + rb]
                y = z if y is None else y + z
            col_blocks.append(y)
        row_blocks.append(jnp.concatenate(col_blocks, axis=1))
    return jnp.concatenate(row_blocks, axis=0) if len(row_blocks) > 1 else row_blocks[0]


def _tail_kernel(x_ref, o_ref, hist_ref,
                 w_glu, w_gate, w_dw, b_dw, lnc_g, lnc_b, w_co, w_ao, w_o,
                 ln1_g, ln1_b, w_up, w_dn, ln2_g, ln2_b,
                 y_ref, cs_ref, u_scr, *, nb, tm):
    t = pl.program_id(1)
    rows = nb * tm

    @pl.when(t == 0)
    def _():
        u_scr[:, 0:HIST_PAD, :] = hist_ref[...]

    x = x_ref[...].reshape(rows, D_MODEL)
    xb = x.astype(BF16)

    g = jnp.dot(xb, w_glu[...], preferred_element_type=F32)
    u = g[:, :C_CONV] * _sigmoid(g[:, C_CONV:])
    u_scr[:, HIST_PAD:HIST_PAD + tm, :] = u.reshape(nb, tm, C_CONV)

    yc = jnp.concatenate([_causal_conv(u_scr, b, w_dw, tm) for b in range(nb)], axis=0)
    u_scr[:, 0:HIST_PAD, :] = u_scr[:, tm:tm + HIST_PAD, :]
    cs_ref[...] = u_scr[:, 0:HIST_PAD, :]

    yc = _layer_norm(yc + b_dw[...], lnc_g[...], lnc_b[...])
    yc = yc * _sigmoid(yc)
    y_conv = jnp.dot(yc.astype(BF16), w_co[...], preferred_element_type=F32)

    o = o_ref[...].reshape(rows, D_ATTN)
    y_attn = jnp.dot(o, w_ao[...], preferred_element_type=F32)

    gates = _sigmoid(jnp.dot(xb, w_gate[...], preferred_element_type=F32))
    merged = gates[:, :D_MODEL] * y_conv + gates[:, D_MODEL:] * y_attn
    mix = jnp.dot(merged.astype(BF16), w_o[...], preferred_element_type=F32)

    x1 = _layer_norm(DN_ALPHA * x + mix, ln1_g[...], ln1_b[...])
    x1b = x1.astype(BF16)
    ff = jnp.zeros((rows, D_MODEL), F32)
    for c in range(D_FF // FF_CHUNK):
        cs = slice(c * FF_CHUNK, (c + 1) * FF_CHUNK)
        h = jnp.maximum(jnp.dot(x1b, w_up[:, cs], preferred_element_type=F32), 0.0)
        ff = ff + jnp.dot((h * h).astype(BF16), w_dn[cs, :], preferred_element_type=F32)
    x2 = _layer_norm(DN_ALPHA * x1 + ff, ln2_g[...], ln2_b[...])
    y_ref[...] = x2.reshape(nb, tm, D_MODEL)


def _tail(x, o, hist, weights, nb, tm):
    bsz, t, _ = x.shape
    kern = functools.partial(_tail_kernel, nb=nb, tm=tm)
    tile = lambda b, i: (b, i, 0)

    def resident(w):
        return pl.BlockSpec(w.shape, lambda b, i: (0,) * w.ndim,
                            pipeline_mode=pl.Buffered(1))

    return pl.pallas_call(
        kern,
        grid=(bsz // nb, t // tm),
        in_specs=[
            pl.BlockSpec((nb, tm, D_MODEL), tile),
            pl.BlockSpec((nb, tm, D_ATTN), tile),
            pl.BlockSpec((nb, HIST_PAD, C_CONV), lambda b, i: (b, 0, 0)),
        ] + [resident(w) for w in weights],
        out_specs=[
            pl.BlockSpec((nb, tm, D_MODEL), tile),
            pl.BlockSpec((nb, HIST_PAD, C_CONV), lambda b, i: (b, 0, 0)),
        ],
        out_shape=[
            jax.ShapeDtypeStruct((bsz, t, D_MODEL), F32),
            jax.ShapeDtypeStruct((bsz, HIST_PAD, C_CONV), F32),
        ],
        scratch_shapes=[pltpu.VMEM((nb, HIST_PAD + tm, C_CONV), F32)],
        compiler_params=pltpu.CompilerParams(
            dimension_semantics=("arbitrary", "arbitrary"),
            vmem_limit_bytes=VMEM_LIMIT),
        name="tail",
    )(x, o, hist, *weights)


def _group(x, hist, past_kt, past_v, past_lft, w_qkvf, b_f, tail_w, *, tm_proj, tq, tk, nb, tm_tail):
    bsz, t, _ = x.shape
    q, k, v, kt, vb, lf = _proj(x, w_qkvf, b_f, tm_proj)
    lft = jnp.swapaxes(lf, 1, 2)
    past = 0
    if past_kt is not None:
        past = past_kt.shape[2]
        length = -(-(past + t) // tk) * tk
        pad = length - past - t
        kt = jnp.pad(jnp.concatenate([past_kt, kt], axis=2), ((0, 0), (0, 0), (0, pad)))
        vb = jnp.pad(jnp.concatenate([past_v, vb], axis=1), ((0, 0), (0, pad), (0, 0)))
        lft = jnp.pad(jnp.concatenate([past_lft, lft], axis=2), ((0, 0), (0, 0), (0, pad)))
    o = _attn(q, kt, vb, lft, tq, tk, past)
    hist = jnp.pad(hist, ((0, 0), (HIST_PAD - HIST, 0), (0, 0)))
    y, cs = _tail(x, o, hist, tail_w, nb, tm_tail)
    hd = (bsz, t, N_HEADS, HEAD_DIM)
    return y, cs[None, :, HIST_PAD - HIST:, :], k.reshape(hd)[None], v.reshape(hd)[None], lf[None]


def kernel(x_prompt, x_sample, cache_conv, cache_k, cache_v, cache_logf, w_in, b_f, w_dw, b_dw,
           ln_conv_g, ln_conv_b, w_conv_out, w_attn_out, w_o, ln1_g, ln1_b, w_up, w_down,
           ln2_g, ln2_b):
    assert w_in.shape[0] == 1, "single layer"
    w = w_in[0]
    row = lambda a: a[0][None, :].astype(F32)
    w_qkvf = jnp.concatenate(
        [w[:, OFF_Q:OFF_K] * (ATTN_SCALE * LOG2E), w[:, OFF_K:OFF_F],
         jnp.pad(w[:, OFF_F:OFF_G], ((0, 0), (0, F_PAD - N_HEADS)))], axis=1).astype(BF16)
    tail_w = (
        w[:, 0:OFF_Q].astype(BF16), w[:, OFF_G:].astype(BF16),
        w_dw[0].astype(F32), row(b_dw), row(ln_conv_g), row(ln_conv_b),
        w_conv_out[0].astype(BF16), w_attn_out[0].astype(BF16), w_o[0].astype(BF16),
        row(ln1_g), row(ln1_b), w_up[0].astype(BF16), w_down[0].astype(BF16),
        row(ln2_g), row(ln2_b))
    bf = row(b_f)

    bp, tp, _ = x_prompt.shape
    bs, ts, _ = x_sample.shape
    yp, cp, kp, vp, lp = _group(
        x_prompt, jnp.zeros((bp, HIST, C_CONV), F32), None, None, None, w_qkvf, bf, tail_w,
        tm_proj=512, tq=512, tk=512, nb=1, tm_tail=512)

    past = cache_k.shape[2]
    past_kt = jnp.swapaxes(cache_k[0].reshape(bs, past, D_ATTN), 1, 2).astype(BF16)
    ones = jnp.zeros((bs, past, N_HEADS, HEAD_BLOCK - HEAD_DIM), BF16).at[..., 0].set(1.0)
    past_v = jnp.concatenate([cache_v[0].astype(BF16), ones], axis=-1).reshape(bs, past, D_BLOCKS)
    past_lft = jnp.swapaxes(cache_logf[0].astype(F32), 1, 2)
    tk_s = -(-(past + ts) // LANES) * LANES
    ys, cs, ks, vs, ls = _group(
        x_sample, cache_conv[0].astype(F32), past_kt, past_v, past_lft, w_qkvf, bf, tail_w,
        tm_proj=ts, tq=ts, tk=tk_s, nb=bs, tm_tail=ts)
    return (yp, ys, cp, kp, vp, lp, cs, ks, vs, ls)
```

```python
import functools

import jax
import jax.numpy as jnp
from jax import lax
from jax.experimental import pallas as pl
from jax.experimental.pallas import tpu as pltpu

D_MODEL = 1024
C_CONV = D_MODEL // 2
CONV_WIDTH = 31
HIST = CONV_WIDTH - 1
HIST_PAD = 32
N_HEADS = 8
HEAD_DIM = 64
D_ATTN = N_HEADS * HEAD_DIM
HEAD_BLOCK = 128
D_BLOCKS = N_HEADS * HEAD_BLOCK
BIAS_ROWS = 3
BF16_ROWS = 16
LOG2E = 1.4426950408889634
D_FF = 4 * D_MODEL
LN_EPS = 1e-5
ATTN_SCALE = HEAD_DIM ** -0.5
NEG_INF = -1e30
DN_ALPHA = 2.0 ** 0.25

OFF_Q = 2 * C_CONV
OFF_K = OFF_Q + D_ATTN
OFF_V = OFF_K + D_ATTN
OFF_F = OFF_V + D_ATTN
OFF_G = OFF_F + N_HEADS

LANES = 128
SUBLANES = 8
CONV_ROWS = 64
CONV_COLS = 256
F_PAD = LANES
FF_CHUNK = 1024
ATTN_AHEAD = 2
TAIL_SPLIT = (1, 4)
VMEM_LIMIT = 58 * 1024 * 1024

F32 = jnp.float32
BF16 = jnp.bfloat16


def _layer_norm(x, g, b):
    mu = jnp.mean(x, axis=-1, keepdims=True)
    xc = x - mu
    var = jnp.mean(xc * xc, axis=-1, keepdims=True)
    return xc * lax.rsqrt(var + LN_EPS) * g + b


def _sigmoid(x):
    return 1.0 / (1.0 + jnp.exp(-x))


def _head_blocks(x, n_ones):
    tm = x.shape[0]
    lane = lax.broadcasted_iota(jnp.int32, (tm, HEAD_BLOCK - HEAD_DIM), 1)
    tail = jnp.where(lane < n_ones, 1.0, 0.0).astype(F32)
    parts = []
    for h in range(N_HEADS):
        parts += [x[:, h * HEAD_DIM:(h + 1) * HEAD_DIM], tail]
    return jnp.concatenate(parts, axis=1).astype(BF16)


def _proj_kernel(x_ref, w_ref, bf_ref, q_ref, k_ref, v_ref, kt_ref, vb_ref, lf_ref):
    x = x_ref[0].astype(BF16)
    p = jnp.dot(x, w_ref[...], preferred_element_type=F32)
    q_ref[0] = _head_blocks(p[:, 0:D_ATTN], BIAS_ROWS)
    k = p[:, D_ATTN:2 * D_ATTN]
    v = p[:, 2 * D_ATTN:3 * D_ATTN]
    k_ref[0] = k
    v_ref[0] = v
    kt_ref[0] = k.T.astype(BF16)
    vb_ref[0] = _head_blocks(v, 1)
    z = p[:, 3 * D_ATTN:3 * D_ATTN + N_HEADS] + bf_ref[...]
    lf_ref[0] = jnp.minimum(z, 0.0) - jnp.log1p(jnp.exp(-jnp.abs(z)))


def _proj(x, w_qkvf, b_f, tm):
    bsz, t, d = x.shape
    grid = (bsz, t // tm)
    n_out = w_qkvf.shape[1]
    tile = lambda b, i: (b, i, 0)
    return pl.pallas_call(
        _proj_kernel,
        grid=grid,
        in_specs=[
            pl.BlockSpec((1, tm, d), tile),
            pl.BlockSpec((d, n_out), lambda b, i: (0, 0)),
            pl.BlockSpec((1, N_HEADS), lambda b, i: (0, 0)),
        ],
        out_specs=[
            pl.BlockSpec((1, tm, D_BLOCKS), tile),
            pl.BlockSpec((1, tm, D_ATTN), tile),
            pl.BlockSpec((1, tm, D_ATTN), tile),
            pl.BlockSpec((1, D_ATTN, tm), lambda b, i: (b, 0, i)),
            pl.BlockSpec((1, tm, D_BLOCKS), tile),
            pl.BlockSpec((1, tm, N_HEADS), tile),
        ],
        out_shape=[
            jax.ShapeDtypeStruct((bsz, t, D_BLOCKS), BF16),
            jax.ShapeDtypeStruct((bsz, t, D_ATTN), F32),
            jax.ShapeDtypeStruct((bsz, t, D_ATTN), F32),
            jax.ShapeDtypeStruct((bsz, D_ATTN, t), BF16),
            jax.ShapeDtypeStruct((bsz, t, D_BLOCKS), BF16),
            jax.ShapeDtypeStruct((bsz, t, N_HEADS), F32),
        ],
        compiler_params=pltpu.CompilerParams(
            dimension_semantics=("arbitrary", "arbitrary"),
            vmem_limit_bytes=VMEM_LIMIT),
        name="proj",
    )(x, w_qkvf, b_f)


def _attn_kernel(q_ref, kt_ref, v_ref, lft_ref, o_ref, kta_scr, s_scr, m_scr, acc_scr,
                 *, tq, tk, past, length):
    i = pl.program_id(1)

    @pl.when(i == 0)
    def _():
        c = lft_ref[0]
        lane = lax.broadcasted_iota(jnp.int32, c.shape, 1)
        shift = 1
        while shift < length:
            c = c + jnp.where(lane >= shift, pltpu.roll(c, shift, 1), 0.0)
            shift *= 2
        cl = c * (-LOG2E)
        hi = cl.astype(BF16).astype(F32)
        mid = (cl - hi).astype(BF16).astype(F32)
        lo = (cl - hi - mid).astype(BF16).astype(F32)
        row = lax.broadcasted_iota(jnp.int32, (BF16_ROWS, length), 0)
        for h in range(N_HEADS):
            base = h * HEAD_BLOCK
            kta_scr[base:base + HEAD_DIM, :] = kt_ref[0, h * HEAD_DIM:(h + 1) * HEAD_DIM, :]
            blk = jnp.where(row == 0, hi[h:h + 1], jnp.where(row == 1, mid[h:h + 1],
                            jnp.where(row == 2, lo[h:h + 1], 0.0)))
            kta_scr[base + HEAD_DIM:base + HEAD_DIM + BF16_ROWS, :] = blk.astype(BF16)
            kta_scr[base + HEAD_DIM + BF16_ROWS:base + HEAD_BLOCK, :] = jnp.zeros(
                (HEAD_BLOCK - HEAD_DIM - BF16_ROWS, length), BF16)

    n_full = (past + i * tq) // tk
    offset = past % tk
    if offset == 0 and tq == tk and tq % (2 * LANES) == 0:
        half = tq // 2
        diag_parts = [(0, half, half, 0), (half, tq, tk, half)]
    else:
        diag_parts = [(0, tq, tk, offset)]

    m_scr[...] = jnp.full(m_scr.shape, NEG_INF, F32)
    acc_scr[...] = jnp.zeros(acc_scr.shape, F32)

    def stages(j, r0, r1, width, mask_offset):
        k0 = pl.multiple_of(j * tk, tk)

        def scores(h):
            hb = slice(h * HEAD_BLOCK, (h + 1) * HEAD_BLOCK)
            s = jnp.dot(q_ref[0, r0:r1, hb], kta_scr[hb, pl.ds(k0, width)],
                        preferred_element_type=F32)
            if mask_offset is not None:
                row = lax.broadcasted_iota(jnp.int32, s.shape, 0)
                col = lax.broadcasted_iota(jnp.int32, s.shape, 1)
                s = jnp.where(col <= row + mask_offset, s, NEG_INF)
            s_scr[h, r0:r1, 0:width] = s
            m_cur = s[:, 0:LANES]
            for g in range(1, width // LANES):
                m_cur = jnp.maximum(m_cur, s[:, g * LANES:(g + 1) * LANES])
            return jnp.max(m_cur, axis=1, keepdims=True)

        def values(h, m_cur):
            hb = slice(h * HEAD_BLOCK, (h + 1) * HEAD_BLOCK)
            m_old = m_scr[h, r0:r1]
            m_new = jnp.maximum(m_old, m_cur)
            m_scr[h, r0:r1] = m_new
            alpha = jnp.exp2(m_old - m_new)
            p = jnp.concatenate(
                [jnp.exp2(s_scr[h, r0:r1, g * LANES:(g + 1) * LANES] - m_new).astype(BF16)
                 for g in range(width // LANES)], axis=1)
            pv = jnp.dot(p, v_ref[0, pl.ds(k0, width), hb], preferred_element_type=F32)
            acc_scr[h, r0:r1] = alpha * acc_scr[h, r0:r1] + pv

        return scores, values

    def full_chunk(j, carry):
        scores, values = stages(j, 0, tq, tk, None)
        row_max = {}
        for h in range(N_HEADS + ATTN_AHEAD):
            if h < N_HEADS:
                row_max[h] = scores(h)
            if h >= ATTN_AHEAD:
                values(h - ATTN_AHEAD, row_max.pop(h - ATTN_AHEAD))
        return carry

    lax.fori_loop(0, n_full, full_chunk, 0)
    diag = [stages(n_full, *part) for part in diag_parts]
    row_max = [[scores(h) for h in range(N_HEADS)] for scores, _ in diag]
    for (_, values), part_max in zip(diag, row_max):
        for h in range(N_HEADS):
            values(h, part_max[h])

    for h in range(0, N_HEADS, 2):
        pair = []
        for hh in (h, h + 1):
            a = acc_scr[hh]
            pair.append(a[:, 0:HEAD_DIM] / a[:, HEAD_DIM:HEAD_DIM + 1])
        o_ref[0, :, h * HEAD_DIM:(h + 2) * HEAD_DIM] = jnp.concatenate(pair, axis=1).astype(BF16)


def _attn(q, kt, v, lft, tq, tk, past):
    bsz, t_q, _ = q.shape
    length = kt.shape[2]
    assert tk % LANES == 0 and length % tk == 0 and past + t_q <= length
    assert (tq % tk == 0 or t_q == tq) and past % tk + tq <= tk
    kern = functools.partial(_attn_kernel, tq=tq, tk=tk, past=past, length=length)
    return pl.pallas_call(
        kern,
        grid=(bsz, t_q // tq),
        in_specs=[
            pl.BlockSpec((1, tq, D_BLOCKS), lambda b, i: (b, i, 0)),
            pl.BlockSpec((1, D_ATTN, length), lambda b, i: (b, 0, 0)),
            pl.BlockSpec((1, length, D_BLOCKS), lambda b, i: (b, 0, 0)),
            pl.BlockSpec((1, N_HEADS, length), lambda b, i: (b, 0, 0)),
        ],
        out_specs=pl.BlockSpec((1, tq, D_ATTN), lambda b, i: (b, i, 0)),
        out_shape=jax.ShapeDtypeStruct((bsz, t_q, D_ATTN), BF16),
        scratch_shapes=[
            pltpu.VMEM((D_BLOCKS, length), BF16),
            pltpu.VMEM((N_HEADS, tq, tk), F32),
            pltpu.VMEM((N_HEADS, tq, LANES), F32),
            pltpu.VMEM((N_HEADS, tq, HEAD_BLOCK), F32),
        ],
        compiler_params=pltpu.CompilerParams(
            dimension_semantics=("arbitrary", "arbitrary"),
            vmem_limit_bytes=VMEM_LIMIT),
        name="attn",
    )(q, kt, v, lft)


def _causal_conv(u_scr, b, w_dw, r_lo, r_hi):
    first = HIST_PAD - HIST
    rb = min(r_hi - r_lo, CONV_ROWS)
    row_blocks = []
    for r0 in range(r_lo, r_hi, rb):
        col_blocks = []
        for c0 in range(0, C_CONV, CONV_COLS):
            cols = slice(c0, c0 + CONV_COLS)
            y = None
            for r in range(SUBLANES):
                off = first + r
                base = off // SUBLANES * SUBLANES
                sh = off - base
                n = rb + (SUBLANES if sh else 0)
                z = None
                for j in range(r, CONV_WIDTH, SUBLANES):
                    lo = r0 + base + j - r
                    term = w_dw[j:j + 1, cols] * u_scr[b, lo:lo + n, cols]
                    z = term if z is None else z + term
                z = z[sh:sh + rb]
                y = z if y is None else y + z
            col_blocks.append(y)
        row_blocks.append(jnp.concatenate(col_blocks, axis=1))
    return jnp.concatenate(row_blocks, axis=0) if len(row_blocks) > 1 else row_blocks[0]


def _tail_kernel(x_ref, o_ref, hist_ref,
                 w_glu, w_gate, w_dw, b_dw, lnc_g, lnc_b, w_co, w_ao, w_o,
                 ln1_g, ln1_b, w_up, w_dn, ln2_g, ln2_b,
                 y_ref, cs_ref, u_scr, *, nb, tm):
    t = pl.program_id(1)

    @pl.when(t == 0)
    def _():
        u_scr[:, 0:HIST_PAD, :] = hist_ref[...]

    if nb == 1 and tm % (TAIL_SPLIT[-1] * CONV_ROWS) == 0:
        bounds = [0] + [tm * f // TAIL_SPLIT[-1] for f in TAIL_SPLIT]
    else:
        bounds = [0, tm]
    n_parts = len(bounds) - 1
    st = [dict() for _ in range(n_parts)]

    def rows_of(p):
        return slice(bounds[p], bounds[p + 1])

    def glu(p):
        x = x_ref[:, rows_of(p), :].reshape(-1, D_MODEL)
        st[p]["x"] = x
        st[p]["xb"] = x.astype(BF16)
        g = jnp.dot(st[p]["xb"], w_glu[...], preferred_element_type=F32)
        u = g[:, :C_CONV] * _sigmoid(g[:, C_CONV:])
        u_scr[:, HIST_PAD + bounds[p]:HIST_PAD + bounds[p + 1], :] = u.reshape(nb, -1, C_CONV)

    def gate_dots(p):
        st[p]["gates"] = jnp.dot(st[p]["xb"], w_gate[...], preferred_element_type=F32)
        o = o_ref[:, rows_of(p), :].reshape(-1, D_ATTN)
        st[p]["y_attn"] = jnp.dot(o, w_ao[...], preferred_element_type=F32)

    def conv(p):
        yc = jnp.concatenate(
            [_causal_conv(u_scr, b, w_dw, bounds[p], bounds[p + 1]) for b in range(nb)], axis=0)
        yc = _layer_norm(yc + b_dw[...], lnc_g[...], lnc_b[...])
        st[p]["yc"] = (yc * _sigmoid(yc)).astype(BF16)

    def conv_out(p):
        st[p]["y_conv"] = jnp.dot(st[p].pop("yc"), w_co[...], preferred_element_type=F32)

    def merge(p):
        gates = _sigmoid(st[p].pop("gates"))
        merged = gates[:, :D_MODEL] * st[p].pop("y_conv") + gates[:, D_MODEL:] * st[p].pop("y_attn")
        st[p]["mix"] = jnp.dot(merged.astype(BF16), w_o[...], preferred_element_type=F32)

    def ffn(p):
        x1 = _layer_norm(DN_ALPHA * st[p].pop("x") + st[p].pop("mix"), ln1_g[...], ln1_b[...])
        x1b = x1.astype(BF16)
        ff = None
        for c in range(D_FF // FF_CHUNK):
            cs = slice(c * FF_CHUNK, (c + 1) * FF_CHUNK)
            h = jnp.maximum(jnp.dot(x1b, w_up[:, cs], preferred_element_type=F32), 0.0)
            d = jnp.dot((h * h).astype(BF16), w_dn[cs, :], preferred_element_type=F32)
            ff = d if ff is None else ff + d
        x2 = _layer_norm(DN_ALPHA * x1 + ff, ln2_g[...], ln2_b[...])
        y_ref[:, rows_of(p), :] = x2.reshape(nb, -1, D_MODEL)

    for p in range(n_parts):
        glu(p)
    for p in range(n_parts):
        gate_dots(p)
    conv(0)
    conv_out(0)
    merge(0)
    for p in range(1, n_parts):
        conv(p)
        ffn(p - 1)
        conv_out(p)
        merge(p)
    ffn(n_parts - 1)

    u_scr[:, 0:HIST_PAD, :] = u_scr[:, tm:tm + HIST_PAD, :]
    cs_ref[...] = u_scr[:, 0:HIST_PAD, :]


def _tail(x, o, hist, weights, nb, tm):
    bsz, t, _ = x.shape
    kern = functools.partial(_tail_kernel, nb=nb, tm=tm)
    tile = lambda b, i: (b, i, 0)

    def resident(w):
        return pl.BlockSpec(w.shape, lambda b, i: (0,) * w.ndim,
                            pipeline_mode=pl.Buffered(1))

    return pl.pallas_call(
        kern,
        grid=(bsz // nb, t // tm),
        in_specs=[
            pl.BlockSpec((nb, tm, D_MODEL), tile),
            pl.BlockSpec((nb, tm, D_ATTN), tile),
            pl.BlockSpec((nb, HIST_PAD, C_CONV), lambda b, i: (b, 0, 0)),
        ] + [resident(w) for w in weights],
        out_specs=[
            pl.BlockSpec((nb, tm, D_MODEL), tile),
            pl.BlockSpec((nb, HIST_PAD, C_CONV), lambda b, i: (b, 0, 0)),
        ],
        out_shape=[
            jax.ShapeDtypeStruct((bsz, t, D_MODEL), F32),
            jax.ShapeDtypeStruct((bsz, HIST_PAD, C_CONV), F32),
        ],
        scratch_shapes=[pltpu.VMEM((nb, HIST_PAD + tm, C_CONV), F32)],
        compiler_params=pltpu.CompilerParams(
            dimension_semantics=("arbitrary", "arbitrary"),
            vmem_limit_bytes=VMEM_LIMIT),
        name="tail",
    )(x, o, hist, *weights)


def _group(x, hist, past_kt, past_v, past_lft, w_qkvf, b_f, tail_w, *, tm_proj, tq, tk, nb, tm_tail):
    bsz, t, _ = x.shape
    q, k, v, kt, vb, lf = _proj(x, w_qkvf, b_f, tm_proj)
    lft = jnp.swapaxes(lf, 1, 2)
    past = 0
    if past_kt is not None:
        past = past_kt.shape[2]
        length = -(-(past + t) // tk) * tk
        pad = length - past - t
        kt = jnp.pad(jnp.concatenate([past_kt, kt], axis=2), ((0, 0), (0, 0), (0, pad)))
        vb = jnp.pad(jnp.concatenate([past_v, vb], axis=1), ((0, 0), (0, pad), (0, 0)))
        lft = jnp.pad(jnp.concatenate([past_lft, lft], axis=2), ((0, 0), (0, 0), (0, pad)))
    o = _attn(q, kt, vb, lft, tq, tk, past)
    hist = jnp.pad(hist, ((0, 0), (HIST_PAD - HIST, 0), (0, 0)))
    y, cs = _tail(x, o, hist, tail_w, nb, tm_tail)
    hd = (bsz, t, N_HEADS, HEAD_DIM)
    return y, cs[None, :, HIST_PAD - HIST:, :], k.reshape(hd)[None], v.reshape(hd)[None], lf[None]


def kernel(x_prompt, x_sample, cache_conv, cache_k, cache_v, cache_logf, w_in, b_f, w_dw, b_dw,
           ln_conv_g, ln_conv_b, w_conv_out, w_attn_out, w_o, ln1_g, ln1_b, w_up, w_down,
           ln2_g, ln2_b):
    assert w_in.shape[0] == 1, "single layer"
    w = w_in[0]
    row = lambda a: a[0][None, :].astype(F32)
    w_qkvf = jnp.concatenate(
        [w[:, OFF_Q:OFF_K] * (ATTN_SCALE * LOG2E), w[:, OFF_K:OFF_F],
         jnp.pad(w[:, OFF_F:OFF_G], ((0, 0), (0, F_PAD - N_HEADS)))], axis=1).astype(BF16)
    tail_w = (
        w[:, 0:OFF_Q].astype(BF16), w[:, OFF_G:].astype(BF16),
        w_dw[0].astype(F32), row(b_dw), row(ln_conv_g), row(ln_conv_b),
        w_conv_out[0].astype(BF16), w_attn_out[0].astype(BF16), w_o[0].astype(BF16),
        row(ln1_g), row(ln1_b), w_up[0].astype(BF16), w_down[0].astype(BF16),
        row(ln2_g), row(ln2_b))
    bf = row(b_f)

    bp, tp, _ = x_prompt.shape
    bs, ts, _ = x_sample.shape
    yp, cp, kp, vp, lp = _group(
        x_prompt, jnp.zeros((bp, HIST, C_CONV), F32), None, None, None, w_qkvf, bf, tail_w,
        tm_proj=512, tq=512, tk=512, nb=1, tm_tail=512)

    past = cache_k.shape[2]
    past_kt = jnp.swapaxes(cache_k[0].reshape(bs, past, D_ATTN), 1, 2).astype(BF16)
    ones = jnp.zeros((bs, past, N_HEADS, HEAD_BLOCK - HEAD_DIM), BF16).at[..., 0].set(1.0)
    past_v = jnp.concatenate([cache_v[0].astype(BF16), ones], axis=-1).reshape(bs, past, D_BLOCKS)
    past_lft = jnp.swapaxes(cache_logf[0].astype(F32), 1, 2)
    tk_s = -(-(past + ts) // LANES) * LANES
    ys, cs, ks, vs, ls = _group(
        x_sample, cache_conv[0].astype(F32), past_kt, past_v, past_lft, w_qkvf, bf, tail_w,
        tm_proj=ts, tq=ts, tk=tk_s, nb=bs, tm_tail=ts)
    return (yp, ys, cp, kp, vp, lp, cs, ks, vs, ls)
```

```python
import functools

import jax
import jax.numpy as jnp
from jax import lax
from jax.experimental import pallas as pl
from jax.experimental.pallas import tpu as pltpu

D_MODEL = 1024
C_CONV = D_MODEL // 2
CONV_WIDTH = 31
HIST = CONV_WIDTH - 1
HIST_PAD = 32
N_HEADS = 8
HEAD_DIM = 64
D_ATTN = N_HEADS * HEAD_DIM
HEAD_BLOCK = 128
D_BLOCKS = N_HEADS * HEAD_BLOCK
BIAS_ROWS = 3
BF16_ROWS = 16
LOG2E = 1.4426950408889634
D_FF = 4 * D_MODEL
LN_EPS = 1e-5
ATTN_SCALE = HEAD_DIM ** -0.5
NEG_INF = -1e30
DN_ALPHA = 2.0 ** 0.25

OFF_Q = 2 * C_CONV
OFF_K = OFF_Q + D_ATTN
OFF_V = OFF_K + D_ATTN
OFF_F = OFF_V + D_ATTN
OFF_G = OFF_F + N_HEADS

LANES = 128
SUBLANES = 8
CONV_ROWS = 64
CONV_COLS = 256
F_PAD = LANES
FF_CHUNK = 1024
ATTN_AHEAD = 2
TAIL_SPLIT = (1, 4)
VMEM_LIMIT = 58 * 1024 * 1024

F32 = jnp.float32
BF16 = jnp.bfloat16


def _layer_norm(x, g, b):
    mu = jnp.mean(x, axis=-1, keepdims=True)
    xc = x - mu
    var = jnp.mean(xc * xc, axis=-1, keepdims=True)
    return xc * lax.rsqrt(var + LN_EPS) * g + b


def _sigmoid(x):
    return 0.5 * jnp.tanh(0.5 * x) + 0.5


def _head_blocks(x, n_ones):
    tm = x.shape[0]
    lane = lax.broadcasted_iota(jnp.int32, (tm, HEAD_BLOCK - HEAD_DIM), 1)
    tail = jnp.where(lane < n_ones, 1.0, 0.0).astype(F32)
    parts = []
    for h in range(N_HEADS):
        parts += [x[:, h * HEAD_DIM:(h + 1) * HEAD_DIM], tail]
    return jnp.concatenate(parts, axis=1).astype(BF16)


def _proj_kernel(x_ref, w_ref, bf_ref, q_ref, k_ref, v_ref, kt_ref, vb_ref, lf_ref):
    x = x_ref[0].astype(BF16)
    p = jnp.dot(x, w_ref[...], preferred_element_type=F32)
    q_ref[0] = _head_blocks(p[:, 0:D_ATTN], BIAS_ROWS)
    k = p[:, D_ATTN:2 * D_ATTN]
    v = p[:, 2 * D_ATTN:3 * D_ATTN]
    k_ref[0] = k
    v_ref[0] = v
    kt_ref[0] = k.T.astype(BF16)
    vb_ref[0] = _head_blocks(v, 1)
    z = p[:, 3 * D_ATTN:3 * D_ATTN + F_PAD] + bf_ref[...]
    lf = jnp.minimum(z, 0.0) - jnp.log1p(jnp.exp(-jnp.abs(z)))
    lf_ref[0] = lf.T[0:N_HEADS, :]


def _proj(x, w_qkvf, b_f, tm):
    bsz, t, d = x.shape
    grid = (bsz, t // tm)
    n_out = w_qkvf.shape[1]
    tile = lambda b, i: (b, i, 0)
    return pl.pallas_call(
        _proj_kernel,
        grid=grid,
        in_specs=[
            pl.BlockSpec((1, tm, d), tile),
            pl.BlockSpec((d, n_out), lambda b, i: (0, 0)),
            pl.BlockSpec((1, F_PAD), lambda b, i: (0, 0)),
        ],
        out_specs=[
            pl.BlockSpec((1, tm, D_BLOCKS), tile),
            pl.BlockSpec((1, tm, D_ATTN), tile),
            pl.BlockSpec((1, tm, D_ATTN), tile),
            pl.BlockSpec((1, D_ATTN, tm), lambda b, i: (b, 0, i)),
            pl.BlockSpec((1, tm, D_BLOCKS), tile),
            pl.BlockSpec((1, N_HEADS, tm), lambda b, i: (b, 0, i)),
        ],
        out_shape=[
            jax.ShapeDtypeStruct((bsz, t, D_BLOCKS), BF16),
            jax.ShapeDtypeStruct((bsz, t, D_ATTN), F32),
            jax.ShapeDtypeStruct((bsz, t, D_ATTN), F32),
            jax.ShapeDtypeStruct((bsz, D_ATTN, t), BF16),
            jax.ShapeDtypeStruct((bsz, t, D_BLOCKS), BF16),
            jax.ShapeDtypeStruct((bsz, N_HEADS, t), F32),
        ],
        compiler_params=pltpu.CompilerParams(
            dimension_semantics=("arbitrary", "arbitrary"),
            vmem_limit_bytes=VMEM_LIMIT),
        name="proj",
    )(x, w_qkvf, b_f)


def _attn_kernel(q_ref, kt_ref, v_ref, lft_ref, o_ref, kta_scr, s_scr, m_scr, acc_scr,
                 *, tq, tk, past, length):
    i = pl.program_id(1)

    @pl.when(i == 0)
    def _():
        c = lft_ref[0]
        lane = lax.broadcasted_iota(jnp.int32, c.shape, 1)
        shift = 1
        while shift < length:
            c = c + jnp.where(lane >= shift, pltpu.roll(c, shift, 1), 0.0)
            shift *= 2
        cl = c * (-LOG2E)
        hi = cl.astype(BF16).astype(F32)
        mid = (cl - hi).astype(BF16).astype(F32)
        lo = (cl - hi - mid).astype(BF16).astype(F32)
        row = lax.broadcasted_iota(jnp.int32, (BF16_ROWS, length), 0)
        for h in range(N_HEADS):
            base = h * HEAD_BLOCK
            kta_scr[base:base + HEAD_DIM, :] = kt_ref[0, h * HEAD_DIM:(h + 1) * HEAD_DIM, :]
            blk = jnp.where(row == 0, hi[h:h + 1], jnp.where(row == 1, mid[h:h + 1],
                            jnp.where(row == 2, lo[h:h + 1], 0.0)))
            kta_scr[base + HEAD_DIM:base + HEAD_DIM + BF16_ROWS, :] = blk.astype(BF16)
            kta_scr[base + HEAD_DIM + BF16_ROWS:base + HEAD_BLOCK, :] = jnp.zeros(
                (HEAD_BLOCK - HEAD_DIM - BF16_ROWS, length), BF16)

    n_full = (past + i * tq) // tk
    offset = past % tk
    if offset == 0 and tq == tk and tq % (2 * LANES) == 0:
        half = tq // 2
        diag_parts = [(0, half, half, 0), (half, tq, tk, half)]
    else:
        diag_parts = [(0, tq, tk, offset)]

    m_scr[...] = jnp.full(m_scr.shape, NEG_INF, F32)
    acc_scr[...] = jnp.zeros(acc_scr.shape, F32)

    def stages(j, r0, r1, width, mask_offset):
        k0 = pl.multiple_of(j * tk, tk)

        def scores(h):
            hb = slice(h * HEAD_BLOCK, (h + 1) * HEAD_BLOCK)
            s = jnp.dot(q_ref[0, r0:r1, hb], kta_scr[hb, pl.ds(k0, width)],
                        preferred_element_type=F32)
            if mask_offset is not None:
                row = lax.broadcasted_iota(jnp.int32, s.shape, 0)
                col = lax.broadcasted_iota(jnp.int32, s.shape, 1)
                s = jnp.where(col <= row + mask_offset, s, NEG_INF)
            s_scr[h, r0:r1, 0:width] = s
            m_cur = s[:, 0:LANES]
            for g in range(1, width // LANES):
                m_cur = jnp.maximum(m_cur, s[:, g * LANES:(g + 1) * LANES])
            return jnp.max(m_cur, axis=1, keepdims=True)

        def values(h, m_cur):
            hb = slice(h * HEAD_BLOCK, (h + 1) * HEAD_BLOCK)
            m_old = m_scr[h, r0:r1]
            m_new = jnp.maximum(m_old, m_cur)
            m_scr[h, r0:r1] = m_new
            alpha = jnp.exp2(m_old - m_new)
            p = jnp.concatenate(
                [jnp.exp2(s_scr[h, r0:r1, g * LANES:(g + 1) * LANES] - m_new).astype(BF16)
                 for g in range(width // LANES)], axis=1)
            pv = jnp.dot(p, v_ref[0, pl.ds(k0, width), hb], preferred_element_type=F32)
            acc_scr[h, r0:r1] = alpha * acc_scr[h, r0:r1] + pv

        return scores, values

    def full_chunk(j, carry):
        scores, values = stages(j, 0, tq, tk, None)
        row_max = {}
        for h in range(N_HEADS + ATTN_AHEAD):
            if h < N_HEADS:
                row_max[h] = scores(h)
            if h >= ATTN_AHEAD:
                values(h - ATTN_AHEAD, row_max.pop(h - ATTN_AHEAD))
        return carry

    lax.fori_loop(0, n_full, full_chunk, 0)
    diag = [stages(n_full, *part) for part in diag_parts]
    row_max = [[scores(h) for h in range(N_HEADS)] for scores, _ in diag]
    for (_, values), part_max in zip(diag, row_max):
        for h in range(N_HEADS):
            values(h, part_max[h])

    for h in range(0, N_HEADS, 2):
        pair = []
        for hh in (h, h + 1):
            a = acc_scr[hh]
            pair.append(a[:, 0:HEAD_DIM] / a[:, HEAD_DIM:HEAD_DIM + 1])
        o_ref[0, :, h * HEAD_DIM:(h + 2) * HEAD_DIM] = jnp.concatenate(pair, axis=1).astype(BF16)


def _attn(q, kt, v, lft, tq, tk, past):
    bsz, t_q, _ = q.shape
    length = kt.shape[2]
    assert tk % LANES == 0 and length % tk == 0 and past + t_q <= length
    assert (tq % tk == 0 or t_q == tq) and past % tk + tq <= tk
    kern = functools.partial(_attn_kernel, tq=tq, tk=tk, past=past, length=length)
    return pl.pallas_call(
        kern,
        grid=(bsz, t_q // tq),
        in_specs=[
            pl.BlockSpec((1, tq, D_BLOCKS), lambda b, i: (b, i, 0)),
            pl.BlockSpec((1, D_ATTN, length), lambda b, i: (b, 0, 0)),
            pl.BlockSpec((1, length, D_BLOCKS), lambda b, i: (b, 0, 0)),
            pl.BlockSpec((1, N_HEADS, length), lambda b, i: (b, 0, 0)),
        ],
        out_specs=pl.BlockSpec((1, tq, D_ATTN), lambda b, i: (b, i, 0)),
        out_shape=jax.ShapeDtypeStruct((bsz, t_q, D_ATTN), BF16),
        scratch_shapes=[
            pltpu.VMEM((D_BLOCKS, length), BF16),
            pltpu.VMEM((N_HEADS, tq, tk), F32),
            pltpu.VMEM((N_HEADS, tq, LANES), F32),
            pltpu.VMEM((N_HEADS, tq, HEAD_BLOCK), F32),
        ],
        compiler_params=pltpu.CompilerParams(
            dimension_semantics=("arbitrary", "arbitrary"),
            vmem_limit_bytes=VMEM_LIMIT),
        name="attn",
    )(q, kt, v, lft)


def _causal_conv(u_scr, b, w_dw, r_lo, r_hi):
    first = HIST_PAD - HIST
    rb = min(r_hi - r_lo, CONV_ROWS)
    row_blocks = []
    for r0 in range(r_lo, r_hi, rb):
        col_blocks = []
        for c0 in range(0, C_CONV, CONV_COLS):
            cols = slice(c0, c0 + CONV_COLS)
            y = None
            for r in range(SUBLANES):
                off = first + r
                base = off // SUBLANES * SUBLANES
                sh = off - base
                n = rb + (SUBLANES if sh else 0)
                z = None
                for j in range(r, CONV_WIDTH, SUBLANES):
                    lo = r0 + base + j - r
                    term = w_dw[j:j + 1, cols] * u_scr[b, lo:lo + n, cols]
                    z = term if z is None else z + term
                z = z[sh:sh + rb]
                y = z if y is None else y + z
            col_blocks.append(y)
        row_blocks.append(jnp.concatenate(col_blocks, axis=1))
    return jnp.concatenate(row_blocks, axis=0) if len(row_blocks) > 1 else row_blocks[0]


def _tail_kernel(x_ref, o_ref, hist_ref,
                 w_glu, w_gate, w_dw, b_dw, lnc_g, lnc_b, w_co, w_ao, w_o,
                 ln1_g, ln1_b, w_up, w_dn, ln2_g, ln2_b,
                 y_ref, cs_ref, u_scr, *, nb, tm):
    t = pl.program_id(1)

    @pl.when(t == 0)
    def _():
        u_scr[:, 0:HIST_PAD, :] = hist_ref[...]

    if nb == 1 and tm % (TAIL_SPLIT[-1] * CONV_ROWS) == 0:
        bounds = [0] + [tm * f // TAIL_SPLIT[-1] for f in TAIL_SPLIT]
    else:
        bounds = [0, tm]
    n_parts = len(bounds) - 1
    st = [dict() for _ in range(n_parts)]

    def rows_of(p):
        return slice(bounds[p], bounds[p + 1])

    def glu(p):
        x = x_ref[:, rows_of(p), :].reshape(-1, D_MODEL)
        st[p]["x"] = x
        st[p]["xb"] = x.astype(BF16)
        g = jnp.dot(st[p]["xb"], w_glu[...], preferred_element_type=F32)
        u = g[:, :C_CONV] * _sigmoid(g[:, C_CONV:])
        u_scr[:, HIST_PAD + bounds[p]:HIST_PAD + bounds[p + 1], :] = u.reshape(nb, -1, C_CONV)

    def gate_dots(p):
        st[p]["gates"] = jnp.dot(st[p]["xb"], w_gate[...], preferred_element_type=F32)
        o = o_ref[:, rows_of(p), :].reshape(-1, D_ATTN)
        st[p]["y_attn"] = jnp.dot(o, w_ao[...], preferred_element_type=F32)

    def conv(p):
        yc = jnp.concatenate(
            [_causal_conv(u_scr, b, w_dw, bounds[p], bounds[p + 1]) for b in range(nb)], axis=0)
        yc = _layer_norm(yc + b_dw[...], lnc_g[...], lnc_b[...])
        st[p]["yc"] = (yc * _sigmoid(yc)).astype(BF16)

    def conv_out(p):
        st[p]["y_conv"] = jnp.dot(st[p].pop("yc"), w_co[...], preferred_element_type=F32)

    def merge(p):
        gates = _sigmoid(st[p].pop("gates"))
        merged = gates[:, :D_MODEL] * st[p].pop("y_conv") + gates[:, D_MODEL:] * st[p].pop("y_attn")
        st[p]["mix"] = jnp.dot(merged.astype(BF16), w_o[...], preferred_element_type=F32)

    def ffn(p):
        x1 = _layer_norm(DN_ALPHA * st[p].pop("x") + st[p].pop("mix"), ln1_g[...], ln1_b[...])
        x1b = x1.astype(BF16)
        ff = None
        for c in range(D_FF // FF_CHUNK):
            cs = slice(c * FF_CHUNK, (c + 1) * FF_CHUNK)
            h = jnp.maximum(jnp.dot(x1b, w_up[:, cs], preferred_element_type=F32), 0.0)
            d = jnp.dot((h * h).astype(BF16), w_dn[cs, :], preferred_element_type=F32)
            ff = d if ff is None else ff + d
        x2 = _layer_norm(DN_ALPHA * x1 + ff, ln2_g[...], ln2_b[...])
        y_ref[:, rows_of(p), :] = x2.reshape(nb, -1, D_MODEL)

    for p in range(n_parts):
        glu(p)
    for p in range(n_parts):
        gate_dots(p)
    conv(0)
    conv_out(0)
    merge(0)
    for p in range(1, n_parts):
        conv(p)
        ffn(p - 1)
        conv_out(p)
        merge(p)
    ffn(n_parts - 1)

    u_scr[:, 0:HIST_PAD, :] = u_scr[:, tm:tm + HIST_PAD, :]
    cs_ref[...] = u_scr[:, 0:HIST_PAD, :]


def _tail(x, o, hist, weights, nb, tm):
    bsz, t, _ = x.shape
    kern = functools.partial(_tail_kernel, nb=nb, tm=tm)
    tile = lambda b, i: (b, i, 0)

    def resident(w):
        return pl.BlockSpec(w.shape, lambda b, i: (0,) * w.ndim,
                            pipeline_mode=pl.Buffered(1))

    return pl.pallas_call(
        kern,
        grid=(bsz // nb, t // tm),
        in_specs=[
            pl.BlockSpec((nb, tm, D_MODEL), tile),
            pl.BlockSpec((nb, tm, D_ATTN), tile),
            pl.BlockSpec((nb, HIST_PAD, C_CONV), lambda b, i: (b, 0, 0)),
        ] + [resident(w) for w in weights],
        out_specs=[
            pl.BlockSpec((nb, tm, D_MODEL), tile),
            pl.BlockSpec((nb, HIST_PAD, C_CONV), lambda b, i: (b, 0, 0)),
        ],
        out_shape=[
            jax.ShapeDtypeStruct((bsz, t, D_MODEL), F32),
            jax.ShapeDtypeStruct((bsz, HIST_PAD, C_CONV), F32),
        ],
        scratch_shapes=[pltpu.VMEM((nb, HIST_PAD + tm, C_CONV), F32)],
        compiler_params=pltpu.CompilerParams(
            dimension_semantics=("arbitrary", "arbitrary"),
            vmem_limit_bytes=VMEM_LIMIT),
        name="tail",
    )(x, o, hist, *weights)


def _group(x, hist, past_kt, past_v, past_lft, w_qkvf, b_f, tail_w, *, tm_proj, tq, tk, nb, tm_tail):
    bsz, t, _ = x.shape
    q, k, v, kt, vb, lft = _proj(x, w_qkvf, b_f, tm_proj)
    lf = jnp.swapaxes(lft, 1, 2)
    past = 0
    if past_kt is not None:
        past = past_kt.shape[2]
        length = -(-(past + t) // tk) * tk
        pad = length - past - t
        kt = jnp.pad(jnp.concatenate([past_kt, kt], axis=2), ((0, 0), (0, 0), (0, pad)))
        vb = jnp.pad(jnp.concatenate([past_v, vb], axis=1), ((0, 0), (0, pad), (0, 0)))
        lft = jnp.pad(jnp.concatenate([past_lft, lft], axis=2), ((0, 0), (0, 0), (0, pad)))
    o = _attn(q, kt, vb, lft, tq, tk, past)
    hist = jnp.pad(hist, ((0, 0), (HIST_PAD - HIST, 0), (0, 0)))
    y, cs = _tail(x, o, hist, tail_w, nb, tm_tail)
    hd = (bsz, t, N_HEADS, HEAD_DIM)
    return y, cs[None, :, HIST_PAD - HIST:, :], k.reshape(hd)[None], v.reshape(hd)[None], lf[None]


def kernel(x_prompt, x_sample, cache_conv, cache_k, cache_v, cache_logf, w_in, b_f, w_dw, b_dw,
           ln_conv_g, ln_conv_b, w_conv_out, w_attn_out, w_o, ln1_g, ln1_b, w_up, w_down,
           ln2_g, ln2_b):
    assert w_in.shape[0] == 1, "single layer"
    w = w_in[0]
    row = lambda a: a[0][None, :].astype(F32)
    w_qkvf = jnp.concatenate(
        [w[:, OFF_Q:OFF_K] * (ATTN_SCALE * LOG2E), w[:, OFF_K:OFF_F],
         jnp.pad(w[:, OFF_F:OFF_G], ((0, 0), (0, F_PAD - N_HEADS)))], axis=1).astype(BF16)
    tail_w = (
        w[:, 0:OFF_Q].astype(BF16), w[:, OFF_G:].astype(BF16),
        w_dw[0].astype(F32), row(b_dw), row(ln_conv_g), row(ln_conv_b),
        w_conv_out[0].astype(BF16), w_attn_out[0].astype(BF16), w_o[0].astype(BF16),
        row(ln1_g), row(ln1_b), w_up[0].astype(BF16), w_down[0].astype(BF16),
        row(ln2_g), row(ln2_b))
    bf = jnp.pad(row(b_f), ((0, 0), (0, F_PAD - N_HEADS)))

    bp, tp, _ = x_prompt.shape
    bs, ts, _ = x_sample.shape
    yp, cp, kp, vp, lp = _group(
        x_prompt, jnp.zeros((bp, HIST, C_CONV), F32), None, None, None, w_qkvf, bf, tail_w,
        tm_proj=512, tq=512, tk=512, nb=1, tm_tail=512)

    past = cache_k.shape[2]
    past_kt = jnp.swapaxes(cache_k[0].reshape(bs, past, D_ATTN), 1, 2).astype(BF16)
    ones = jnp.zeros((bs, past, N_HEADS, HEAD_BLOCK - HEAD_DIM), BF16).at[..., 0].set(1.0)
    past_v = jnp.concatenate([cache_v[0].astype(BF16), ones], axis=-1).reshape(bs, past, D_BLOCKS)
    past_lft = jnp.swapaxes(cache_logf[0].astype(F32), 1, 2)
    tk_s = -(-(past + ts) // LANES) * LANES
    ys, cs, ks, vs, ls = _group(
        x_sample, cache_conv[0].astype(F32), past_kt, past_v, past_lft, w_qkvf, bf, tail_w,
        tm_proj=ts, tq=ts, tk=tk_s, nb=bs, tm_tail=ts)
    return (yp, ys, cp, kp, vp, lp, cs, ks, vs, ls)
```

```python
import functools

import jax
import jax.numpy as jnp
from jax import lax
from jax.experimental import pallas as pl
from jax.experimental.pallas import tpu as pltpu

D_MODEL = 1024
C_CONV = D_MODEL // 2
CONV_WIDTH = 31
HIST = CONV_WIDTH - 1
HIST_PAD = 32
N_HEADS = 8
HEAD_DIM = 64
D_ATTN = N_HEADS * HEAD_DIM
HEAD_BLOCK = 128
D_BLOCKS = N_HEADS * HEAD_BLOCK
BIAS_ROWS = 3
BF16_ROWS = 16
LOG2E = 1.4426950408889634
D_FF = 4 * D_MODEL
LN_EPS = 1e-5
ATTN_SCALE = HEAD_DIM ** -0.5
NEG_INF = -1e30
DN_ALPHA = 2.0 ** 0.25

OFF_Q = 2 * C_CONV
OFF_K = OFF_Q + D_ATTN
OFF_V = OFF_K + D_ATTN
OFF_F = OFF_V + D_ATTN
OFF_G = OFF_F + N_HEADS

LANES = 128
SUBLANES = 8
CONV_ROWS = 64
CONV_COLS = 256
F_PAD = LANES
FF_CHUNK = 2048
ATTN_AHEAD = 2
TAIL_SPLIT = (1, 4)
VMEM_LIMIT = 58 * 1024 * 1024

F32 = jnp.float32
BF16 = jnp.bfloat16


def _layer_norm(x, g, b):
    mu = jnp.mean(x, axis=-1, keepdims=True)
    xc = x - mu
    var = jnp.mean(xc * xc, axis=-1, keepdims=True)
    return xc * lax.rsqrt(var + LN_EPS) * g + b


def _sigmoid(x):
    return 0.5 * jnp.tanh(0.5 * x) + 0.5


def _head_blocks(x, n_ones):
    tm = x.shape[0]
    lane = lax.broadcasted_iota(jnp.int32, (tm, HEAD_BLOCK - HEAD_DIM), 1)
    tail = jnp.where(lane < n_ones, 1.0, 0.0).astype(F32)
    parts = []
    for h in range(N_HEADS):
        parts += [x[:, h * HEAD_DIM:(h + 1) * HEAD_DIM], tail]
    return jnp.concatenate(parts, axis=1).astype(BF16)


def _proj_kernel(x_ref, w_ref, bf_ref, q_ref, k_ref, v_ref, kt_ref, vb_ref, lf_ref):
    x = x_ref[0].astype(BF16)
    p = jnp.dot(x, w_ref[...], preferred_element_type=F32)
    q_ref[0] = _head_blocks(p[:, 0:D_ATTN], BIAS_ROWS)
    k = p[:, D_ATTN:2 * D_ATTN]
    v = p[:, 2 * D_ATTN:3 * D_ATTN]
    k_ref[0] = k
    v_ref[0] = v
    kt_ref[0] = k.T.astype(BF16)
    vb_ref[0] = _head_blocks(v, 1)
    z = p[:, 3 * D_ATTN:3 * D_ATTN + F_PAD] + bf_ref[...]
    lf = jnp.minimum(z, 0.0) - jnp.log1p(jnp.exp(-jnp.abs(z)))
    lf_ref[0] = lf.T[0:N_HEADS, :]


def _proj(x, w_qkvf, b_f, tm):
    bsz, t, d = x.shape
    grid = (bsz, t // tm)
    n_out = w_qkvf.shape[1]
    tile = lambda b, i: (b, i, 0)
    return pl.pallas_call(
        _proj_kernel,
        grid=grid,
        in_specs=[
            pl.BlockSpec((1, tm, d), tile),
            pl.BlockSpec((d, n_out), lambda b, i: (0, 0)),
            pl.BlockSpec((1, F_PAD), lambda b, i: (0, 0)),
        ],
        out_specs=[
            pl.BlockSpec((1, tm, D_BLOCKS), tile),
            pl.BlockSpec((1, tm, D_ATTN), tile),
            pl.BlockSpec((1, tm, D_ATTN), tile),
            pl.BlockSpec((1, D_ATTN, tm), lambda b, i: (b, 0, i)),
            pl.BlockSpec((1, tm, D_BLOCKS), tile),
            pl.BlockSpec((1, N_HEADS, tm), lambda b, i: (b, 0, i)),
        ],
        out_shape=[
            jax.ShapeDtypeStruct((bsz, t, D_BLOCKS), BF16),
            jax.ShapeDtypeStruct((bsz, t, D_ATTN), F32),
            jax.ShapeDtypeStruct((bsz, t, D_ATTN), F32),
            jax.ShapeDtypeStruct((bsz, D_ATTN, t), BF16),
            jax.ShapeDtypeStruct((bsz, t, D_BLOCKS), BF16),
            jax.ShapeDtypeStruct((bsz, N_HEADS, t), F32),
        ],
        compiler_params=pltpu.CompilerParams(
            dimension_semantics=("arbitrary", "arbitrary"),
            vmem_limit_bytes=VMEM_LIMIT),
        name="proj",
    )(x, w_qkvf, b_f)


def _attn_kernel(q_ref, kt_ref, v_ref, lft_ref, o_ref, kta_scr, s_scr, m_scr, acc_scr,
                 *, tq, tk, past, length):
    i = pl.program_id(1)

    @pl.when(i == 0)
    def _():
        c = lft_ref[0]
        lane = lax.broadcasted_iota(jnp.int32, c.shape, 1)
        shift = 1
        while shift < length:
            c = c + jnp.where(lane >= shift, pltpu.roll(c, shift, 1), 0.0)
            shift *= 2
        cl = c * (-LOG2E)
        hi = cl.astype(BF16).astype(F32)
        mid = (cl - hi).astype(BF16).astype(F32)
        lo = (cl - hi - mid).astype(BF16).astype(F32)
        row = lax.broadcasted_iota(jnp.int32, (BF16_ROWS, length), 0)
        for h in range(N_HEADS):
            base = h * HEAD_BLOCK
            kta_scr[base:base + HEAD_DIM, :] = kt_ref[0, h * HEAD_DIM:(h + 1) * HEAD_DIM, :]
            blk = jnp.where(row == 0, hi[h:h + 1], jnp.where(row == 1, mid[h:h + 1],
                            jnp.where(row == 2, lo[h:h + 1], 0.0)))
            kta_scr[base + HEAD_DIM:base + HEAD_DIM + BF16_ROWS, :] = blk.astype(BF16)
            kta_scr[base + HEAD_DIM + BF16_ROWS:base + HEAD_BLOCK, :] = jnp.zeros(
                (HEAD_BLOCK - HEAD_DIM - BF16_ROWS, length), BF16)

    n_full = (past + i * tq) // tk
    offset = past % tk
    if offset == 0 and tq == tk and tq % (2 * LANES) == 0:
        half = tq // 2
        diag_parts = [(0, half, half, 0), (half, tq, tk, half)]
    else:
        diag_parts = [(0, tq, tk, offset)]

    m_scr[...] = jnp.full(m_scr.shape, NEG_INF, F32)
    acc_scr[...] = jnp.zeros(acc_scr.shape, F32)

    def stages(j, r0, r1, width, mask_offset):
        k0 = pl.multiple_of(j * tk, tk)

        def scores(h):
            hb = slice(h * HEAD_BLOCK, (h + 1) * HEAD_BLOCK)
            s = jnp.dot(q_ref[0, r0:r1, hb], kta_scr[hb, pl.ds(k0, width)],
                        preferred_element_type=F32)
            if mask_offset is not None:
                row = lax.broadcasted_iota(jnp.int32, s.shape, 0)
                col = lax.broadcasted_iota(jnp.int32, s.shape, 1)
                s = jnp.where(col <= row + mask_offset, s, NEG_INF)
            s_scr[h, r0:r1, 0:width] = s
            m_cur = s[:, 0:LANES]
            for g in range(1, width // LANES):
                m_cur = jnp.maximum(m_cur, s[:, g * LANES:(g + 1) * LANES])
            return jnp.max(m_cur, axis=1, keepdims=True)

        def values(h, m_cur):
            hb = slice(h * HEAD_BLOCK, (h + 1) * HEAD_BLOCK)
            m_old = m_scr[h, r0:r1]
            m_new = jnp.maximum(m_old, m_cur)
            m_scr[h, r0:r1] = m_new
            alpha = jnp.exp2(m_old - m_new)
            p = jnp.concatenate(
                [jnp.exp2(s_scr[h, r0:r1, g * LANES:(g + 1) * LANES] - m_new).astype(BF16)
                 for g in range(width // LANES)], axis=1)
            pv = jnp.dot(p, v_ref[0, pl.ds(k0, width), hb], preferred_element_type=F32)
            acc_scr[h, r0:r1] = alpha * acc_scr[h, r0:r1] + pv

        return scores, values

    def full_chunk(j, carry):
        scores, values = stages(j, 0, tq, tk, None)
        row_max = {}
        for h in range(N_HEADS + ATTN_AHEAD):
            if h < N_HEADS:
                row_max[h] = scores(h)
            if h >= ATTN_AHEAD:
                values(h - ATTN_AHEAD, row_max.pop(h - ATTN_AHEAD))
        return carry

    lax.fori_loop(0, n_full, full_chunk, 0)
    diag = [stages(n_full, *part) for part in diag_parts]
    row_max = [[scores(h) for h in range(N_HEADS)] for scores, _ in diag]
    for (_, values), part_max in zip(diag, row_max):
        for h in range(N_HEADS):
            values(h, part_max[h])

    for h in range(0, N_HEADS, 2):
        pair = []
        for hh in (h, h + 1):
            a = acc_scr[hh]
            pair.append(a[:, 0:HEAD_DIM] / a[:, HEAD_DIM:HEAD_DIM + 1])
        o_ref[0, :, h * HEAD_DIM:(h + 2) * HEAD_DIM] = jnp.concatenate(pair, axis=1).astype(BF16)


def _attn(q, kt, v, lft, tq, tk, past):
    bsz, t_q, _ = q.shape
    length = kt.shape[2]
    assert tk % LANES == 0 and length % tk == 0 and past + t_q <= length
    assert (tq % tk == 0 or t_q == tq) and past % tk + tq <= tk
    kern = functools.partial(_attn_kernel, tq=tq, tk=tk, past=past, length=length)
    return pl.pallas_call(
        kern,
        grid=(bsz, t_q // tq),
        in_specs=[
            pl.BlockSpec((1, tq, D_BLOCKS), lambda b, i: (b, i, 0)),
            pl.BlockSpec((1, D_ATTN, length), lambda b, i: (b, 0, 0)),
            pl.BlockSpec((1, length, D_BLOCKS), lambda b, i: (b, 0, 0)),
            pl.BlockSpec((1, N_HEADS, length), lambda b, i: (b, 0, 0)),
        ],
        out_specs=pl.BlockSpec((1, tq, D_ATTN), lambda b, i: (b, i, 0)),
        out_shape=jax.ShapeDtypeStruct((bsz, t_q, D_ATTN), BF16),
        scratch_shapes=[
            pltpu.VMEM((D_BLOCKS, length), BF16),
            pltpu.VMEM((N_HEADS, tq, tk), F32),
            pltpu.VMEM((N_HEADS, tq, LANES), F32),
            pltpu.VMEM((N_HEADS, tq, HEAD_BLOCK), F32),
        ],
        compiler_params=pltpu.CompilerParams(
            dimension_semantics=("arbitrary", "arbitrary"),
            vmem_limit_bytes=VMEM_LIMIT),
        name="attn",
    )(q, kt, v, lft)


def _causal_conv(u_scr, b, w_dw, r_lo, r_hi):
    first = HIST_PAD - HIST
    rb = min(r_hi - r_lo, CONV_ROWS)
    row_blocks = []
    for r0 in range(r_lo, r_hi, rb):
        col_blocks = []
        for c0 in range(0, C_CONV, CONV_COLS):
            cols = slice(c0, c0 + CONV_COLS)
            y = None
            for r in range(SUBLANES):
                off = first + r
                base = off // SUBLANES * SUBLANES
                sh = off - base
                n = rb + (SUBLANES if sh else 0)
                z = None
                for j in range(r, CONV_WIDTH, SUBLANES):
                    lo = r0 + base + j - r
                    term = w_dw[j:j + 1, cols] * u_scr[b, lo:lo + n, cols]
                    z = term if z is None else z + term
                z = z[sh:sh + rb]
                y = z if y is None else y + z
            col_blocks.append(y)
        row_blocks.append(jnp.concatenate(col_blocks, axis=1))
    return jnp.concatenate(row_blocks, axis=0) if len(row_blocks) > 1 else row_blocks[0]


def _tail_kernel(x_ref, o_ref, hist_ref,
                 w_glu, w_gate, w_dw, b_dw, lnc_g, lnc_b, w_co, w_ao, w_o,
                 ln1_g, ln1_b, w_up, w_dn, ln2_g, ln2_b,
                 y_ref, cs_ref, u_scr, *, nb, tm):
    t = pl.program_id(1)

    @pl.when(t == 0)
    def _():
        u_scr[:, 0:HIST_PAD, :] = hist_ref[...]

    if nb == 1 and tm % (TAIL_SPLIT[-1] * CONV_ROWS) == 0:
        bounds = [0] + [tm * f // TAIL_SPLIT[-1] for f in TAIL_SPLIT]
    else:
        bounds = [0, tm]
    n_parts = len(bounds) - 1
    st = [dict() for _ in range(n_parts)]

    def rows_of(p):
        return slice(bounds[p], bounds[p + 1])

    def glu(p):
        x = x_ref[:, rows_of(p), :].reshape(-1, D_MODEL)
        st[p]["x"] = x
        st[p]["xb"] = x.astype(BF16)
        g = jnp.dot(st[p]["xb"], w_glu[...], preferred_element_type=F32)
        u = g[:, :C_CONV] * _sigmoid(g[:, C_CONV:])
        u_scr[:, HIST_PAD + bounds[p]:HIST_PAD + bounds[p + 1], :] = u.reshape(nb, -1, C_CONV)

    def gate_dots(p):
        st[p]["gates"] = jnp.dot(st[p]["xb"], w_gate[...], preferred_element_type=F32)
        o = o_ref[:, rows_of(p), :].reshape(-1, D_ATTN)
        st[p]["y_attn"] = jnp.dot(o, w_ao[...], preferred_element_type=F32)

    def conv(p):
        yc = jnp.concatenate(
            [_causal_conv(u_scr, b, w_dw, bounds[p], bounds[p + 1]) for b in range(nb)], axis=0)
        yc = _layer_norm(yc + b_dw[...], lnc_g[...], lnc_b[...])
        st[p]["yc"] = (yc * _sigmoid(yc)).astype(BF16)

    def conv_out(p):
        st[p]["y_conv"] = jnp.dot(st[p].pop("yc"), w_co[...], preferred_element_type=F32)

    def merge(p):
        gates = _sigmoid(st[p].pop("gates"))
        merged = gates[:, :D_MODEL] * st[p].pop("y_conv") + gates[:, D_MODEL:] * st[p].pop("y_attn")
        st[p]["mix"] = jnp.dot(merged.astype(BF16), w_o[...], preferred_element_type=F32)

    def ffn(p):
        x1 = _layer_norm(DN_ALPHA * st[p].pop("x") + st[p].pop("mix"), ln1_g[...], ln1_b[...])
        x1b = x1.astype(BF16)
        ff = None
        for c in range(D_FF // FF_CHUNK):
            cs = slice(c * FF_CHUNK, (c + 1) * FF_CHUNK)
            h = jnp.maximum(jnp.dot(x1b, w_up[:, cs], preferred_element_type=F32), 0.0)
            d = jnp.dot((h * h).astype(BF16), w_dn[cs, :], preferred_element_type=F32)
            ff = d if ff is None else ff + d
        x2 = _layer_norm(DN_ALPHA * x1 + ff, ln2_g[...], ln2_b[...])
        y_ref[:, rows_of(p), :] = x2.reshape(nb, -1, D_MODEL)

    for p in range(n_parts):
        glu(p)
    for p in range(n_parts):
        gate_dots(p)
    conv(0)
    conv_out(0)
    merge(0)
    for p in range(1, n_parts):
        conv(p)
        ffn(p - 1)
        conv_out(p)
        merge(p)
    ffn(n_parts - 1)

    u_scr[:, 0:HIST_PAD, :] = u_scr[:, tm:tm + HIST_PAD, :]
    cs_ref[...] = u_scr[:, 0:HIST_PAD, :]


def _tail(x, o, hist, weights, nb, tm):
    bsz, t, _ = x.shape
    kern = functools.partial(_tail_kernel, nb=nb, tm=tm)
    tile = lambda b, i: (b, i, 0)

    def resident(w):
        return pl.BlockSpec(w.shape, lambda b, i: (0,) * w.ndim,
                            pipeline_mode=pl.Buffered(1))

    return pl.pallas_call(
        kern,
        grid=(bsz // nb, t // tm),
        in_specs=[
            pl.BlockSpec((nb, tm, D_MODEL), tile),
            pl.BlockSpec((nb, tm, D_ATTN), tile),
            pl.BlockSpec((nb, HIST_PAD, C_CONV), lambda b, i: (b, 0, 0)),
        ] + [resident(w) for w in weights],
        out_specs=[
            pl.BlockSpec((nb, tm, D_MODEL), tile),
            pl.BlockSpec((nb, HIST_PAD, C_CONV), lambda b, i: (b, 0, 0)),
        ],
        out_shape=[
            jax.ShapeDtypeStruct((bsz, t, D_MODEL), F32),
            jax.ShapeDtypeStruct((bsz, HIST_PAD, C_CONV), F32),
        ],
        scratch_shapes=[pltpu.VMEM((nb, HIST_PAD + tm, C_CONV), F32)],
        compiler_params=pltpu.CompilerParams(
            dimension_semantics=("arbitrary", "arbitrary"),
            vmem_limit_bytes=VMEM_LIMIT),
        name="tail",
    )(x, o, hist, *weights)


def _group(x, hist, past_kt, past_v, past_lft, w_qkvf, b_f, tail_w, *, tm_proj, tq, tk, nb, tm_tail):
    bsz, t, _ = x.shape
    q, k, v, kt, vb, lft = _proj(x, w_qkvf, b_f, tm_proj)
    lf = jnp.swapaxes(lft, 1, 2)
    past = 0
    if past_kt is not None:
        past = past_kt.shape[2]
        length = -(-(past + t) // tk) * tk
        pad = length - past - t
        kt = jnp.pad(jnp.concatenate([past_kt, kt], axis=2), ((0, 0), (0, 0), (0, pad)))
        vb = jnp.pad(jnp.concatenate([past_v, vb], axis=1), ((0, 0), (0, pad), (0, 0)))
        lft = jnp.pad(jnp.concatenate([past_lft, lft], axis=2), ((0, 0), (0, 0), (0, pad)))
    o = _attn(q, kt, vb, lft, tq, tk, past)
    hist = jnp.pad(hist, ((0, 0), (HIST_PAD - HIST, 0), (0, 0)))
    y, cs = _tail(x, o, hist, tail_w, nb, tm_tail)
    hd = (bsz, t, N_HEADS, HEAD_DIM)
    return y, cs[None, :, HIST_PAD - HIST:, :], k.reshape(hd)[None], v.reshape(hd)[None], lf[None]


def kernel(x_prompt, x_sample, cache_conv, cache_k, cache_v, cache_logf, w_in, b_f, w_dw, b_dw,
           ln_conv_g, ln_conv_b, w_conv_out, w_attn_out, w_o, ln1_g, ln1_b, w_up, w_down,
           ln2_g, ln2_b):
    assert w_in.shape[0] == 1, "single layer"
    w = w_in[0]
    row = lambda a: a[0][None, :].astype(F32)
    w_qkvf = jnp.concatenate(
        [w[:, OFF_Q:OFF_K] * (ATTN_SCALE * LOG2E), w[:, OFF_K:OFF_F],
         jnp.pad(w[:, OFF_F:OFF_G], ((0, 0), (0, F_PAD - N_HEADS)))], axis=1).astype(BF16)
    tail_w = (
        w[:, 0:OFF_Q].astype(BF16), w[:, OFF_G:].astype(BF16),
        w_dw[0].astype(F32), row(b_dw), row(ln_conv_g), row(ln_conv_b),
        w_conv_out[0].astype(BF16), w_attn_out[0].astype(BF16), w_o[0].astype(BF16),
        row(ln1_g), row(ln1_b), w_up[0].astype(BF16), w_down[0].astype(BF16),
        row(ln2_g), row(ln2_b))
    bf = jnp.pad(row(b_f), ((0, 0), (0, F_PAD - N_HEADS)))

    bp, tp, _ = x_prompt.shape
    bs, ts, _ = x_sample.shape
    yp, cp, kp, vp, lp = _group(
        x_prompt, jnp.zeros((bp, HIST, C_CONV), F32), None, None, None, w_qkvf, bf, tail_w,
        tm_proj=1024, tq=512, tk=512, nb=1, tm_tail=512)

    past = cache_k.shape[2]
    past_kt = jnp.swapaxes(cache_k[0].reshape(bs, past, D_ATTN), 1, 2).astype(BF16)
    ones = jnp.zeros((bs, past, N_HEADS, HEAD_BLOCK - HEAD_DIM), BF16).at[..., 0].set(1.0)
    past_v = jnp.concatenate([cache_v[0].astype(BF16), ones], axis=-1).reshape(bs, past, D_BLOCKS)
    past_lft = jnp.swapaxes(cache_logf[0].astype(F32), 1, 2)
    tk_s = -(-(past + ts) // LANES) * LANES
    ys, cs, ks, vs, ls = _group(
        x_sample, cache_conv[0].astype(F32), past_kt, past_v, past_lft, w_qkvf, bf, tail_w,
        tm_proj=ts, tq=ts, tk=tk_s, nb=bs, tm_tail=ts)
    return (yp, ys, cp, kp, vp, lp, cs, ks, vs, ls)
```

```python
import functools

import jax
import jax.numpy as jnp
from jax import lax
from jax.experimental import pallas as pl
from jax.experimental.pallas import tpu as pltpu

D_MODEL = 1024
C_CONV = D_MODEL // 2
CONV_WIDTH = 31
HIST = CONV_WIDTH - 1
HIST_PAD = 32
N_HEADS = 8
HEAD_DIM = 64
D_ATTN = N_HEADS * HEAD_DIM
HEAD_BLOCK = 128
D_BLOCKS = N_HEADS * HEAD_BLOCK
BIAS_ROWS = 3
BF16_ROWS = 16
LOG2E = 1.4426950408889634
D_FF = 4 * D_MODEL
LN_EPS = 1e-5
ATTN_SCALE = HEAD_DIM ** -0.5
NEG_INF = -1e30
DN_ALPHA = 2.0 ** 0.25

OFF_Q = 2 * C_CONV
OFF_K = OFF_Q + D_ATTN
OFF_V = OFF_K + D_ATTN
OFF_F = OFF_V + D_ATTN
OFF_G = OFF_F + N_HEADS

LANES = 128
SUBLANES = 8
CONV_ROWS = 64
CONV_COLS = 256
F_PAD = LANES
FF_CHUNK = 2048
ATTN_AHEAD = 2
TAIL_SPLIT = (1, 4)
VMEM_LIMIT = 58 * 1024 * 1024

F32 = jnp.float32
BF16 = jnp.bfloat16


def _layer_norm(x, g, b):
    mu = jnp.mean(x, axis=-1, keepdims=True)
    xc = x - mu
    var = jnp.mean(xc * xc, axis=-1, keepdims=True)
    return xc * lax.rsqrt(var + LN_EPS) * g + b


def _sigmoid(x):
    return 0.5 * jnp.tanh(0.5 * x) + 0.5


def _head_blocks(x, n_ones):
    tm = x.shape[0]
    lane = lax.broadcasted_iota(jnp.int32, (tm, HEAD_BLOCK - HEAD_DIM), 1)
    tail = jnp.where(lane < n_ones, 1.0, 0.0).astype(F32)
    parts = []
    for h in range(N_HEADS):
        parts += [x[:, h * HEAD_DIM:(h + 1) * HEAD_DIM], tail]
    return jnp.concatenate(parts, axis=1).astype(BF16)


def _proj_kernel(x_ref, w_ref, bf_ref, q_ref, k_ref, v_ref, kt_ref, vb_ref, lf_ref):
    x = x_ref[0].astype(BF16)
    p = jnp.dot(x, w_ref[...], preferred_element_type=F32)
    q_ref[0] = _head_blocks(p[:, 0:D_ATTN], BIAS_ROWS)
    k = p[:, D_ATTN:2 * D_ATTN]
    v = p[:, 2 * D_ATTN:3 * D_ATTN]
    k_ref[0] = k
    v_ref[0] = v
    kt_ref[0] = k.T.astype(BF16)
    vb_ref[0] = _head_blocks(v, 1)
    z = p[:, 3 * D_ATTN:3 * D_ATTN + F_PAD] + bf_ref[...]
    lf = jnp.minimum(z, 0.0) - jnp.log1p(jnp.exp(-jnp.abs(z)))
    lf_ref[0] = lf.T[0:N_HEADS, :]


def _proj(x, w_qkvf, b_f, tm):
    bsz, t, d = x.shape
    grid = (bsz, t // tm)
    n_out = w_qkvf.shape[1]
    tile = lambda b, i: (b, i, 0)
    return pl.pallas_call(
        _proj_kernel,
        grid=grid,
        in_specs=[
            pl.BlockSpec((1, tm, d), tile),
            pl.BlockSpec((d, n_out), lambda b, i: (0, 0)),
            pl.BlockSpec((1, F_PAD), lambda b, i: (0, 0)),
        ],
        out_specs=[
            pl.BlockSpec((1, tm, D_BLOCKS), tile),
            pl.BlockSpec((1, tm, D_ATTN), tile),
            pl.BlockSpec((1, tm, D_ATTN), tile),
            pl.BlockSpec((1, D_ATTN, tm), lambda b, i: (b, 0, i)),
            pl.BlockSpec((1, tm, D_BLOCKS), tile),
            pl.BlockSpec((1, N_HEADS, tm), lambda b, i: (b, 0, i)),
        ],
        out_shape=[
            jax.ShapeDtypeStruct((bsz, t, D_BLOCKS), BF16),
            jax.ShapeDtypeStruct((bsz, t, D_ATTN), F32),
            jax.ShapeDtypeStruct((bsz, t, D_ATTN), F32),
            jax.ShapeDtypeStruct((bsz, D_ATTN, t), BF16),
            jax.ShapeDtypeStruct((bsz, t, D_BLOCKS), BF16),
            jax.ShapeDtypeStruct((bsz, N_HEADS, t), F32),
        ],
        compiler_params=pltpu.CompilerParams(
            dimension_semantics=("arbitrary", "arbitrary"),
            vmem_limit_bytes=VMEM_LIMIT),
        name="proj",
    )(x, w_qkvf, b_f)


def _attn_kernel(*refs, tq, tk, past, length, split):
    if split:
        (q_ref, kt_ref, ktn_ref, v_ref, vn_ref, lft_ref, o_ref,
         kta_scr, s_scr, m_scr, acc_scr, v_scr) = refs
        t_new = ktn_ref.shape[2]
    else:
        q_ref, kt_ref, v_ref, lft_ref, o_ref, kta_scr, s_scr, m_scr, acc_scr = refs
    i = pl.program_id(1)

    @pl.when(i == 0)
    def _():
        c = lft_ref[0]
        lane = lax.broadcasted_iota(jnp.int32, c.shape, 1)
        shift = 1
        while shift < length:
            c = c + jnp.where(lane >= shift, pltpu.roll(c, shift, 1), 0.0)
            shift *= 2
        cl = c * (-LOG2E)
        hi = cl.astype(BF16).astype(F32)
        mid = (cl - hi).astype(BF16).astype(F32)
        lo = (cl - hi - mid).astype(BF16).astype(F32)
        row = lax.broadcasted_iota(jnp.int32, (BF16_ROWS, length), 0)
        for h in range(N_HEADS):
            base = h * HEAD_BLOCK
            hd = slice(h * HEAD_DIM, (h + 1) * HEAD_DIM)
            if split:
                kta_scr[base:base + HEAD_DIM, 0:past] = kt_ref[0, hd, :]
                kta_scr[base:base + HEAD_DIM, past:past + t_new] = ktn_ref[0, hd, :]
                kta_scr[base:base + HEAD_DIM, past + t_new:length] = jnp.zeros(
                    (HEAD_DIM, length - past - t_new), BF16)
            else:
                kta_scr[base:base + HEAD_DIM, :] = kt_ref[0, hd, :]
            blk = jnp.where(row == 0, hi[h:h + 1], jnp.where(row == 1, mid[h:h + 1],
                            jnp.where(row == 2, lo[h:h + 1], 0.0)))
            kta_scr[base + HEAD_DIM:base + HEAD_DIM + BF16_ROWS, :] = blk.astype(BF16)
            kta_scr[base + HEAD_DIM + BF16_ROWS:base + HEAD_BLOCK, :] = jnp.zeros(
                (HEAD_BLOCK - HEAD_DIM - BF16_ROWS, length), BF16)

        if split:
            v_scr[0:past, :] = v_ref[0]
            v_scr[past:past + t_new, :] = vn_ref[0]
            v_scr[past + t_new:length, :] = jnp.zeros((length - past - t_new, D_BLOCKS), BF16)

    n_full = (past + i * tq) // tk
    offset = past % tk
    if offset == 0 and tq == tk and tq % (2 * LANES) == 0:
        half = tq // 2
        diag_parts = [(0, half, half, 0), (half, tq, tk, half)]
    else:
        diag_parts = [(0, tq, tk, offset)]

    m_scr[...] = jnp.full(m_scr.shape, NEG_INF, F32)
    acc_scr[...] = jnp.zeros(acc_scr.shape, F32)

    def stages(j, r0, r1, width, mask_offset):
        k0 = pl.multiple_of(j * tk, tk)

        def scores(h):
            hb = slice(h * HEAD_BLOCK, (h + 1) * HEAD_BLOCK)
            s = jnp.dot(q_ref[0, r0:r1, hb], kta_scr[hb, pl.ds(k0, width)],
                        preferred_element_type=F32)
            if mask_offset is not None:
                row = lax.broadcasted_iota(jnp.int32, s.shape, 0)
                col = lax.broadcasted_iota(jnp.int32, s.shape, 1)
                s = jnp.where(col <= row + mask_offset, s, NEG_INF)
            s_scr[h, r0:r1, 0:width] = s
            m_cur = s[:, 0:LANES]
            for g in range(1, width // LANES):
                m_cur = jnp.maximum(m_cur, s[:, g * LANES:(g + 1) * LANES])
            return jnp.max(m_cur, axis=1, keepdims=True)

        def values(h, m_cur):
            hb = slice(h * HEAD_BLOCK, (h + 1) * HEAD_BLOCK)
            m_old = m_scr[h, r0:r1]
            m_new = jnp.maximum(m_old, m_cur)
            m_scr[h, r0:r1] = m_new
            alpha = jnp.exp2(m_old - m_new)
            p = jnp.concatenate(
                [jnp.exp2(s_scr[h, r0:r1, g * LANES:(g + 1) * LANES] - m_new).astype(BF16)
                 for g in range(width // LANES)], axis=1)
            v_blk = v_scr[pl.ds(k0, width), hb] if split else v_ref[0, pl.ds(k0, width), hb]
            pv = jnp.dot(p, v_blk, preferred_element_type=F32)
            acc_scr[h, r0:r1] = alpha * acc_scr[h, r0:r1] + pv

        return scores, values

    def full_chunk(j, carry):
        scores, values = stages(j, 0, tq, tk, None)
        row_max = {}
        for h in range(N_HEADS + ATTN_AHEAD):
            if h < N_HEADS:
                row_max[h] = scores(h)
            if h >= ATTN_AHEAD:
                values(h - ATTN_AHEAD, row_max.pop(h - ATTN_AHEAD))
        return carry

    lax.fori_loop(0, n_full, full_chunk, 0)
    diag = [stages(n_full, *part) for part in diag_parts]
    row_max = [[scores(h) for h in range(N_HEADS)] for scores, _ in diag]
    for (_, values), part_max in zip(diag, row_max):
        for h in range(N_HEADS):
            values(h, part_max[h])

    for h in range(0, N_HEADS, 2):
        pair = []
        for hh in (h, h + 1):
            a = acc_scr[hh]
            pair.append(a[:, 0:HEAD_DIM] / a[:, HEAD_DIM:HEAD_DIM + 1])
        o_ref[0, :, h * HEAD_DIM:(h + 2) * HEAD_DIM] = jnp.concatenate(pair, axis=1).astype(BF16)


def _attn(q, kt, v, lft, tq, tk, past, kt_new=None, v_new=None):
    bsz, t_q, _ = q.shape
    split = kt_new is not None
    length = lft.shape[2]
    assert tk % LANES == 0 and length % tk == 0 and past + t_q <= length
    assert (tq % tk == 0 or t_q == tq) and past % tk + tq <= tk
    kern = functools.partial(_attn_kernel, tq=tq, tk=tk, past=past, length=length, split=split)
    seq = lambda b, i: (b, 0, 0)
    kv_specs = [pl.BlockSpec((1,) + kt.shape[1:], seq)]
    kv_args = [kt]
    if split:
        kv_specs.append(pl.BlockSpec((1,) + kt_new.shape[1:], seq))
        kv_args.append(kt_new)
    kv_specs.append(pl.BlockSpec((1,) + v.shape[1:], seq))
    kv_args.append(v)
    if split:
        kv_specs.append(pl.BlockSpec((1,) + v_new.shape[1:], seq))
        kv_args.append(v_new)
    scratch = [
        pltpu.VMEM((D_BLOCKS, length), BF16),
        pltpu.VMEM((N_HEADS, tq, tk), F32),
        pltpu.VMEM((N_HEADS, tq, LANES), F32),
        pltpu.VMEM((N_HEADS, tq, HEAD_BLOCK), F32),
    ]
    if split:
        scratch.append(pltpu.VMEM((length, D_BLOCKS), BF16))
    return pl.pallas_call(
        kern,
        grid=(bsz, t_q // tq),
        in_specs=[pl.BlockSpec((1, tq, D_BLOCKS), lambda b, i: (b, i, 0))] + kv_specs + [
            pl.BlockSpec((1, N_HEADS, length), seq)],
        out_specs=pl.BlockSpec((1, tq, D_ATTN), lambda b, i: (b, i, 0)),
        out_shape=jax.ShapeDtypeStruct((bsz, t_q, D_ATTN), BF16),
        scratch_shapes=scratch,
        compiler_params=pltpu.CompilerParams(
            dimension_semantics=("arbitrary", "arbitrary"),
            vmem_limit_bytes=VMEM_LIMIT),
        name="attn",
    )(q, *kv_args, lft)


def _causal_conv(u_scr, b, w_dw, r_lo, r_hi):
    first = HIST_PAD - HIST
    rb = min(r_hi - r_lo, CONV_ROWS)
    row_blocks = []
    for r0 in range(r_lo, r_hi, rb):
        col_blocks = []
        for c0 in range(0, C_CONV, CONV_COLS):
            cols = slice(c0, c0 + CONV_COLS)
            y = None
            for r in range(SUBLANES):
                off = first + r
                base = off // SUBLANES * SUBLANES
                sh = off - base
                n = rb + (SUBLANES if sh else 0)
                z = None
                for j in range(r, CONV_WIDTH, SUBLANES):
                    lo = r0 + base + j - r
                    term = w_dw[j:j + 1, cols] * u_scr[b, lo:lo + n, cols]
                    z = term if z is None else z + term
                z = z[sh:sh + rb]
                y = z if y is None else y + z
            col_blocks.append(y)
        row_blocks.append(jnp.concatenate(col_blocks, axis=1))
    return jnp.concatenate(row_blocks, axis=0) if len(row_blocks) > 1 else row_blocks[0]


def _tail_kernel(x_ref, o_ref, hist_ref,
                 w_glu, w_gate, w_dw, b_dw, lnc_g, lnc_b, w_co, w_ao, w_o,
                 ln1_g, ln1_b, w_up, w_dn, ln2_g, ln2_b,
                 y_ref, cs_ref, u_scr, *, nb, tm):
    t = pl.program_id(1)

    @pl.when(t == 0)
    def _():
        u_scr[:, 0:HIST_PAD, :] = hist_ref[...]

    if nb == 1 and tm % (TAIL_SPLIT[-1] * CONV_ROWS) == 0:
        bounds = [0] + [tm * f // TAIL_SPLIT[-1] for f in TAIL_SPLIT]
    else:
        bounds = [0, tm]
    n_parts = len(bounds) - 1
    st = [dict() for _ in range(n_parts)]

    def rows_of(p):
        return slice(bounds[p], bounds[p + 1])

    def glu(p):
        x = x_ref[:, rows_of(p), :].reshape(-1, D_MODEL)
        st[p]["x"] = x
        st[p]["xb"] = x.astype(BF16)
        g = jnp.dot(st[p]["xb"], w_glu[...], preferred_element_type=F32)
        u = g[:, :C_CONV] * _sigmoid(g[:, C_CONV:])
        u_scr[:, HIST_PAD + bounds[p]:HIST_PAD + bounds[p + 1], :] = u.reshape(nb, -1, C_CONV)

    def gate_dots(p):
        st[p]["gates"] = jnp.dot(st[p]["xb"], w_gate[...], preferred_element_type=F32)
        o = o_ref[:, rows_of(p), :].reshape(-1, D_ATTN)
        st[p]["y_attn"] = jnp.dot(o, w_ao[...], preferred_element_type=F32)

    def conv(p):
        yc = jnp.concatenate(
            [_causal_conv(u_scr, b, w_dw, bounds[p], bounds[p + 1]) for b in range(nb)], axis=0)
        yc = _layer_norm(yc + b_dw[...], lnc_g[...], lnc_b[...])
        st[p]["yc"] = (yc * _sigmoid(yc)).astype(BF16)

    def conv_out(p):
        st[p]["y_conv"] = jnp.dot(st[p].pop("yc"), w_co[...], preferred_element_type=F32)

    def merge(p):
        gates = _sigmoid(st[p].pop("gates"))
        merged = gates[:, :D_MODEL] * st[p].pop("y_conv") + gates[:, D_MODEL:] * st[p].pop("y_attn")
        st[p]["mix"] = jnp.dot(merged.astype(BF16), w_o[...], preferred_element_type=F32)

    def ffn(p):
        x1 = _layer_norm(DN_ALPHA * st[p].pop("x") + st[p].pop("mix"), ln1_g[...], ln1_b[...])
        x1b = x1.astype(BF16)
        ff = None
        for c in range(D_FF // FF_CHUNK):
            cs = slice(c * FF_CHUNK, (c + 1) * FF_CHUNK)
            h = jnp.maximum(jnp.dot(x1b, w_up[:, cs], preferred_element_type=F32), 0.0)
            d = jnp.dot((h * h).astype(BF16), w_dn[cs, :], preferred_element_type=F32)
            ff = d if ff is None else ff + d
        x2 = _layer_norm(DN_ALPHA * x1 + ff, ln2_g[...], ln2_b[...])
        y_ref[:, rows_of(p), :] = x2.reshape(nb, -1, D_MODEL)

    for p in range(n_parts):
        glu(p)
    for p in range(n_parts):
        gate_dots(p)
    conv(0)
    conv_out(0)
    merge(0)
    for p in range(1, n_parts):
        conv(p)
        ffn(p - 1)
        conv_out(p)
        merge(p)
    ffn(n_parts - 1)

    u_scr[:, 0:HIST_PAD, :] = u_scr[:, tm:tm + HIST_PAD, :]
    cs_ref[...] = u_scr[:, 0:HIST_PAD, :]


def _tail(x, o, hist, weights, nb, tm):
    bsz, t, _ = x.shape
    kern = functools.partial(_tail_kernel, nb=nb, tm=tm)
    tile = lambda b, i: (b, i, 0)

    def resident(w):
        return pl.BlockSpec(w.shape, lambda b, i: (0,) * w.ndim,
                            pipeline_mode=pl.Buffered(1))

    return pl.pallas_call(
        kern,
        grid=(bsz // nb, t // tm),
        in_specs=[
            pl.BlockSpec((nb, tm, D_MODEL), tile),
            pl.BlockSpec((nb, tm, D_ATTN), tile),
            pl.BlockSpec((nb, HIST_PAD, C_CONV), lambda b, i: (b, 0, 0)),
        ] + [resident(w) for w in weights],
        out_specs=[
            pl.BlockSpec((nb, tm, D_MODEL), tile),
            pl.BlockSpec((nb, HIST_PAD, C_CONV), lambda b, i: (b, 0, 0)),
        ],
        out_shape=[
            jax.ShapeDtypeStruct((bsz, t, D_MODEL), F32),
            jax.ShapeDtypeStruct((bsz, HIST_PAD, C_CONV), F32),
        ],
        scratch_shapes=[pltpu.VMEM((nb, HIST_PAD + tm, C_CONV), F32)],
        compiler_params=pltpu.CompilerParams(
            dimension_semantics=("arbitrary", "arbitrary"),
            vmem_limit_bytes=VMEM_LIMIT),
        name="tail",
    )(x, o, hist, *weights)


def _group(x, hist, past_kt, past_v, past_lft, w_qkvf, b_f, tail_w, *, tm_proj, tq, tk, nb, tm_tail):
    bsz, t, _ = x.shape
    q, k, v, kt, vb, lft = _proj(x, w_qkvf, b_f, tm_proj)
    lf = jnp.swapaxes(lft, 1, 2)
    if past_kt is not None:
        past = past_kt.shape[2]
        pad = -(-(past + t) // tk) * tk - past - t
        lft_all = jnp.pad(jnp.concatenate([past_lft, lft], axis=2), ((0, 0), (0, 0), (0, pad)))
        o = _attn(q, past_kt, past_v, lft_all, tq, tk, past, kt_new=kt, v_new=vb)
    else:
        o = _attn(q, kt, vb, lft, tq, tk, 0)
    hist = jnp.pad(hist, ((0, 0), (HIST_PAD - HIST, 0), (0, 0)))
    y, cs = _tail(x, o, hist, tail_w, nb, tm_tail)
    hd = (bsz, t, N_HEADS, HEAD_DIM)
    return y, cs[None, :, HIST_PAD - HIST:, :], k.reshape(hd)[None], v.reshape(hd)[None], lf[None]


def kernel(x_prompt, x_sample, cache_conv, cache_k, cache_v, cache_logf, w_in, b_f, w_dw, b_dw,
           ln_conv_g, ln_conv_b, w_conv_out, w_attn_out, w_o, ln1_g, ln1_b, w_up, w_down,
           ln2_g, ln2_b):
    assert w_in.shape[0] == 1, "single layer"
    w = w_in[0]
    row = lambda a: a[0][None, :].astype(F32)
    w_qkvf = jnp.concatenate(
        [w[:, OFF_Q:OFF_K] * (ATTN_SCALE * LOG2E), w[:, OFF_K:OFF_F],
         jnp.pad(w[:, OFF_F:OFF_G], ((0, 0), (0, F_PAD - N_HEADS)))], axis=1).astype(BF16)
    tail_w = (
        w[:, 0:OFF_Q].astype(BF16), w[:, OFF_G:].astype(BF16),
        w_dw[0].astype(F32), row(b_dw), row(ln_conv_g), row(ln_conv_b),
        w_conv_out[0].astype(BF16), w_attn_out[0].astype(BF16), w_o[0].astype(BF16),
        row(ln1_g), row(ln1_b), w_up[0].astype(BF16), w_down[0].astype(BF16),
        row(ln2_g), row(ln2_b))
    bf = jnp.pad(row(b_f), ((0, 0), (0, F_PAD - N_HEADS)))

    bp, tp, _ = x_prompt.shape
    bs, ts, _ = x_sample.shape
    yp, cp, kp, vp, lp = _group(
        x_prompt, jnp.zeros((bp, HIST, C_CONV), F32), None, None, None, w_qkvf, bf, tail_w,
        tm_proj=1024, tq=512, tk=512, nb=1, tm_tail=512)

    past = cache_k.shape[2]
    past_kt = jnp.swapaxes(cache_k[0].reshape(bs, past, D_ATTN), 1, 2).astype(BF16)
    ones = jnp.zeros((bs, past, N_HEADS, HEAD_BLOCK - HEAD_DIM), BF16).at[..., 0].set(1.0)
    past_v = jnp.concatenate([cache_v[0].astype(BF16), ones], axis=-1).reshape(bs, past, D_BLOCKS)
    past_lft = jnp.swapaxes(cache_logf[0].astype(F32), 1, 2)
    tk_s = -(-(past + ts) // LANES) * LANES
    ys, cs, ks, vs, ls = _group(
        x_sample, cache_conv[0].astype(F32), past_kt, past_v, past_lft, w_qkvf, bf, tail_w,
        tm_proj=ts, tq=ts, tk=tk_s, nb=bs, tm_tail=ts)
    return (yp, ys, cp, kp, vp, lp, cs, ks, vs, ls)
```

```python
import functools

import jax
import jax.numpy as jnp
from jax import lax
from jax.experimental import pallas as pl
from jax.experimental.pallas import tpu as pltpu

D_MODEL = 1024
C_CONV = D_MODEL // 2
CONV_WIDTH = 31
HIST = CONV_WIDTH - 1
HIST_PAD = 32
N_HEADS = 8
HEAD_DIM = 64
D_ATTN = N_HEADS * HEAD_DIM
HEAD_BLOCK = 128
D_BLOCKS = N_HEADS * HEAD_BLOCK
BIAS_ROWS = 3
BF16_ROWS = 16
LOG2E = 1.4426950408889634
D_FF = 4 * D_MODEL
LN_EPS = 1e-5
ATTN_SCALE = HEAD_DIM ** -0.5
NEG_INF = -1e30
DN_ALPHA = 2.0 ** 0.25

OFF_Q = 2 * C_CONV
OFF_K = OFF_Q + D_ATTN
OFF_V = OFF_K + D_ATTN
OFF_F = OFF_V + D_ATTN
OFF_G = OFF_F + N_HEADS

LANES = 128
SUBLANES = 8
CONV_ROWS = 64
CONV_COLS = 256
F_PAD = LANES
FF_CHUNK = 2048
ATTN_AHEAD = 2
TAIL_SPLIT = (1, 4)
VMEM_LIMIT = 58 * 1024 * 1024

F32 = jnp.float32
BF16 = jnp.bfloat16


def _layer_norm(x, g, b):
    mu = jnp.mean(x, axis=-1, keepdims=True)
    xc = x - mu
    var = jnp.mean(xc * xc, axis=-1, keepdims=True)
    return xc * lax.rsqrt(var + LN_EPS) * g + b


def _sigmoid(x):
    return 0.5 * jnp.tanh(0.5 * x) + 0.5


def _head_blocks(x, n_ones):
    tm = x.shape[0]
    lane = lax.broadcasted_iota(jnp.int32, (tm, HEAD_BLOCK - HEAD_DIM), 1)
    tail = jnp.where(lane < n_ones, 1.0, 0.0).astype(F32)
    parts = []
    for h in range(N_HEADS):
        parts += [x[:, h * HEAD_DIM:(h + 1) * HEAD_DIM], tail]
    return jnp.concatenate(parts, axis=1).astype(BF16)


def _proj_kernel(x_ref, w_ref, bf_ref, q_ref, k_ref, v_ref, kt_ref, vb_ref, lf_ref):
    x = x_ref[0].astype(BF16)
    p = jnp.dot(x, w_ref[...], preferred_element_type=F32)
    q_ref[0] = _head_blocks(p[:, 0:D_ATTN], BIAS_ROWS)
    k = p[:, D_ATTN:2 * D_ATTN]
    v = p[:, 2 * D_ATTN:3 * D_ATTN]
    k_ref[0] = k
    v_ref[0] = v
    kt_ref[0] = k.T.astype(BF16)
    vb_ref[0] = _head_blocks(v, 1)
    z = p[:, 3 * D_ATTN:3 * D_ATTN + F_PAD] + bf_ref[...]
    lf = jnp.minimum(z, 0.0) - jnp.log1p(jnp.exp(-jnp.abs(z)))
    lf_ref[0] = lf.T[0:N_HEADS, :]


def _proj(x, w_qkvf, b_f, tm):
    bsz, t, d = x.shape
    grid = (bsz, t // tm)
    n_out = w_qkvf.shape[1]
    tile = lambda b, i: (b, i, 0)
    return pl.pallas_call(
        _proj_kernel,
        grid=grid,
        in_specs=[
            pl.BlockSpec((1, tm, d), tile),
            pl.BlockSpec((d, n_out), lambda b, i: (0, 0)),
            pl.BlockSpec((1, F_PAD), lambda b, i: (0, 0)),
        ],
        out_specs=[
            pl.BlockSpec((1, tm, D_BLOCKS), tile),
            pl.BlockSpec((1, tm, D_ATTN), tile),
            pl.BlockSpec((1, tm, D_ATTN), tile),
            pl.BlockSpec((1, D_ATTN, tm), lambda b, i: (b, 0, i)),
            pl.BlockSpec((1, tm, D_BLOCKS), tile),
            pl.BlockSpec((1, N_HEADS, tm), lambda b, i: (b, 0, i)),
        ],
        out_shape=[
            jax.ShapeDtypeStruct((bsz, t, D_BLOCKS), BF16),
            jax.ShapeDtypeStruct((bsz, t, D_ATTN), F32),
            jax.ShapeDtypeStruct((bsz, t, D_ATTN), F32),
            jax.ShapeDtypeStruct((bsz, D_ATTN, t), BF16),
            jax.ShapeDtypeStruct((bsz, t, D_BLOCKS), BF16),
            jax.ShapeDtypeStruct((bsz, N_HEADS, t), F32),
        ],
        compiler_params=pltpu.CompilerParams(
            dimension_semantics=("arbitrary", "arbitrary"),
            vmem_limit_bytes=VMEM_LIMIT),
        name="proj",
    )(x, w_qkvf, b_f)


def _attn_kernel(*refs, tq, tk, past, length, split):
    if split:
        (q_ref, kt_ref, ktn_ref, v_ref, vn_ref, lft_ref, o_ref,
         kta_scr, s_scr, m_scr, acc_scr, v_scr) = refs
        t_new = ktn_ref.shape[2]
    else:
        q_ref, kt_ref, v_ref, lft_ref, o_ref, kta_scr, s_scr, m_scr, acc_scr = refs
    i = pl.program_id(1)

    @pl.when(i == 0)
    def _():
        c = lft_ref[0]
        lane = lax.broadcasted_iota(jnp.int32, c.shape, 1)
        shift = 1
        while shift < length:
            c = c + jnp.where(lane >= shift, pltpu.roll(c, shift, 1), 0.0)
            shift *= 2
        cl = c * (-LOG2E)
        hi = cl.astype(BF16).astype(F32)
        mid = (cl - hi).astype(BF16).astype(F32)
        lo = (cl - hi - mid).astype(BF16).astype(F32)
        row = lax.broadcasted_iota(jnp.int32, (BF16_ROWS, length), 0)
        for h in range(N_HEADS):
            base = h * HEAD_BLOCK
            hd = slice(h * HEAD_DIM, (h + 1) * HEAD_DIM)
            if split:
                kta_scr[base:base + HEAD_DIM, 0:past] = kt_ref[0, hd, :]
                kta_scr[base:base + HEAD_DIM, past:past + t_new] = ktn_ref[0, hd, :]
                kta_scr[base:base + HEAD_DIM, past + t_new:length] = jnp.zeros(
                    (HEAD_DIM, length - past - t_new), BF16)
            else:
                kta_scr[base:base + HEAD_DIM, :] = kt_ref[0, hd, :]
            blk = jnp.where(row == 0, hi[h:h + 1], jnp.where(row == 1, mid[h:h + 1],
                            jnp.where(row == 2, lo[h:h + 1], 0.0)))
            kta_scr[base + HEAD_DIM:base + HEAD_DIM + BF16_ROWS, :] = blk.astype(BF16)
            kta_scr[base + HEAD_DIM + BF16_ROWS:base + HEAD_BLOCK, :] = jnp.zeros(
                (HEAD_BLOCK - HEAD_DIM - BF16_ROWS, length), BF16)

        if split:
            lane = lax.broadcasted_iota(jnp.int32, (past, HEAD_BLOCK - HEAD_DIM), 1)
            ones = jnp.where(lane == 0, 1.0, 0.0).astype(BF16)
            for h in range(N_HEADS):
                base = h * HEAD_BLOCK
                v_scr[0:past, base:base + HEAD_DIM] = v_ref[0, :, h * HEAD_DIM:(h + 1) * HEAD_DIM]
                v_scr[0:past, base + HEAD_DIM:base + HEAD_BLOCK] = ones
            v_scr[past:past + t_new, :] = vn_ref[0]
            v_scr[past + t_new:length, :] = jnp.zeros((length - past - t_new, D_BLOCKS), BF16)

    n_full = (past + i * tq) // tk
    offset = past % tk
    if offset == 0 and tq == tk and tq % (2 * LANES) == 0:
        half = tq // 2
        diag_parts = [(0, half, half, 0), (half, tq, tk, half)]
    else:
        diag_parts = [(0, tq, tk, offset)]

    m_scr[...] = jnp.full(m_scr.shape, NEG_INF, F32)
    acc_scr[...] = jnp.zeros(acc_scr.shape, F32)

    def stages(j, r0, r1, width, mask_offset):
        k0 = pl.multiple_of(j * tk, tk)

        def scores(h):
            hb = slice(h * HEAD_BLOCK, (h + 1) * HEAD_BLOCK)
            s = jnp.dot(q_ref[0, r0:r1, hb], kta_scr[hb, pl.ds(k0, width)],
                        preferred_element_type=F32)
            if mask_offset is not None:
                row = lax.broadcasted_iota(jnp.int32, s.shape, 0)
                col = lax.broadcasted_iota(jnp.int32, s.shape, 1)
                s = jnp.where(col <= row + mask_offset, s, NEG_INF)
            s_scr[h, r0:r1, 0:width] = s
            m_cur = s[:, 0:LANES]
            for g in range(1, width // LANES):
                m_cur = jnp.maximum(m_cur, s[:, g * LANES:(g + 1) * LANES])
            return jnp.max(m_cur, axis=1, keepdims=True)

        def values(h, m_cur):
            hb = slice(h * HEAD_BLOCK, (h + 1) * HEAD_BLOCK)
            m_old = m_scr[h, r0:r1]
            m_new = jnp.maximum(m_old, m_cur)
            m_scr[h, r0:r1] = m_new
            alpha = jnp.exp2(m_old - m_new)
            p = jnp.concatenate(
                [jnp.exp2(s_scr[h, r0:r1, g * LANES:(g + 1) * LANES] - m_new).astype(BF16)
                 for g in range(width // LANES)], axis=1)
            v_blk = v_scr[pl.ds(k0, width), hb] if split else v_ref[0, pl.ds(k0, width), hb]
            pv = jnp.dot(p, v_blk, preferred_element_type=F32)
            acc_scr[h, r0:r1] = alpha * acc_scr[h, r0:r1] + pv

        return scores, values

    def full_chunk(j, carry):
        scores, values = stages(j, 0, tq, tk, None)
        row_max = {}
        for h in range(N_HEADS + ATTN_AHEAD):
            if h < N_HEADS:
                row_max[h] = scores(h)
            if h >= ATTN_AHEAD:
                values(h - ATTN_AHEAD, row_max.pop(h - ATTN_AHEAD))
        return carry

    lax.fori_loop(0, n_full, full_chunk, 0)
    diag = [stages(n_full, *part) for part in diag_parts]
    row_max = [[scores(h) for h in range(N_HEADS)] for scores, _ in diag]
    for (_, values), part_max in zip(diag, row_max):
        for h in range(N_HEADS):
            values(h, part_max[h])

    for h in range(0, N_HEADS, 2):
        pair = []
        for hh in (h, h + 1):
            a = acc_scr[hh]
            pair.append(a[:, 0:HEAD_DIM] / a[:, HEAD_DIM:HEAD_DIM + 1])
        o_ref[0, :, h * HEAD_DIM:(h + 2) * HEAD_DIM] = jnp.concatenate(pair, axis=1).astype(BF16)


def _attn(q, kt, v, lft, tq, tk, past, kt_new=None, v_new=None):
    bsz, t_q, _ = q.shape
    split = kt_new is not None
    length = lft.shape[2]
    assert tk % LANES == 0 and length % tk == 0 and past + t_q <= length
    assert (tq % tk == 0 or t_q == tq) and past % tk + tq <= tk
    kern = functools.partial(_attn_kernel, tq=tq, tk=tk, past=past, length=length, split=split)
    seq = lambda b, i: (b, 0, 0)
    kv_specs = [pl.BlockSpec((1,) + kt.shape[1:], seq)]
    kv_args = [kt]
    if split:
        kv_specs.append(pl.BlockSpec((1,) + kt_new.shape[1:], seq))
        kv_args.append(kt_new)
    kv_specs.append(pl.BlockSpec((1,) + v.shape[1:], seq))
    kv_args.append(v)
    if split:
        kv_specs.append(pl.BlockSpec((1,) + v_new.shape[1:], seq))
        kv_args.append(v_new)
    scratch = [
        pltpu.VMEM((D_BLOCKS, length), BF16),
        pltpu.VMEM((N_HEADS, tq, tk), F32),
        pltpu.VMEM((N_HEADS, tq, LANES), F32),
        pltpu.VMEM((N_HEADS, tq, HEAD_BLOCK), F32),
    ]
    if split:
        scratch.append(pltpu.VMEM((length, D_BLOCKS), BF16))
    return pl.pallas_call(
        kern,
        grid=(bsz, t_q // tq),
        in_specs=[pl.BlockSpec((1, tq, D_BLOCKS), lambda b, i: (b, i, 0))] + kv_specs + [
            pl.BlockSpec((1, N_HEADS, length), seq)],
        out_specs=pl.BlockSpec((1, tq, D_ATTN), lambda b, i: (b, i, 0)),
        out_shape=jax.ShapeDtypeStruct((bsz, t_q, D_ATTN), BF16),
        scratch_shapes=scratch,
        compiler_params=pltpu.CompilerParams(
            dimension_semantics=("arbitrary", "arbitrary"),
            vmem_limit_bytes=VMEM_LIMIT),
        name="attn",
    )(q, *kv_args, lft)


def _causal_conv(u_scr, b, w_dw, r_lo, r_hi):
    first = HIST_PAD - HIST
    rb = min(r_hi - r_lo, CONV_ROWS)
    row_blocks = []
    for r0 in range(r_lo, r_hi, rb):
        col_blocks = []
        for c0 in range(0, C_CONV, CONV_COLS):
            cols = slice(c0, c0 + CONV_COLS)
            y = None
            for r in range(SUBLANES):
                off = first + r
                base = off // SUBLANES * SUBLANES
                sh = off - base
                n = rb + (SUBLANES if sh else 0)
                z = None
                for j in range(r, CONV_WIDTH, SUBLANES):
                    lo = r0 + base + j - r
                    term = w_dw[j:j + 1, cols] * u_scr[b, lo:lo + n, cols]
                    z = term if z is None else z + term
                z = z[sh:sh + rb]
                y = z if y is None else y + z
            col_blocks.append(y)
        row_blocks.append(jnp.concatenate(col_blocks, axis=1))
    return jnp.concatenate(row_blocks, axis=0) if len(row_blocks) > 1 else row_blocks[0]


def _tail_kernel(x_ref, o_ref, hist_ref,
                 w_glu, w_gate, w_dw, b_dw, lnc_g, lnc_b, w_co, w_ao, w_o,
                 ln1_g, ln1_b, w_up, w_dn, ln2_g, ln2_b,
                 y_ref, cs_ref, u_scr, *, nb, tm):
    t = pl.program_id(1)

    @pl.when(t == 0)
    def _():
        u_scr[:, 0:HIST_PAD, :] = hist_ref[...]

    if nb == 1 and tm % (TAIL_SPLIT[-1] * CONV_ROWS) == 0:
        bounds = [0] + [tm * f // TAIL_SPLIT[-1] for f in TAIL_SPLIT]
    else:
        bounds = [0, tm]
    n_parts = len(bounds) - 1
    st = [dict() for _ in range(n_parts)]

    def rows_of(p):
        return slice(bounds[p], bounds[p + 1])

    def glu(p):
        x = x_ref[:, rows_of(p), :].reshape(-1, D_MODEL)
        st[p]["x"] = x
        st[p]["xb"] = x.astype(BF16)
        g = jnp.dot(st[p]["xb"], w_glu[...], preferred_element_type=F32)
        u = g[:, :C_CONV] * _sigmoid(g[:, C_CONV:])
        u_scr[:, HIST_PAD + bounds[p]:HIST_PAD + bounds[p + 1], :] = u.reshape(nb, -1, C_CONV)

    def gate_dots(p):
        st[p]["gates"] = jnp.dot(st[p]["xb"], w_gate[...], preferred_element_type=F32)
        o = o_ref[:, rows_of(p), :].reshape(-1, D_ATTN)
        st[p]["y_attn"] = jnp.dot(o, w_ao[...], preferred_element_type=F32)

    def conv(p):
        yc = jnp.concatenate(
            [_causal_conv(u_scr, b, w_dw, bounds[p], bounds[p + 1]) for b in range(nb)], axis=0)
        yc = _layer_norm(yc + b_dw[...], lnc_g[...], lnc_b[...])
        st[p]["yc"] = (yc * _sigmoid(yc)).astype(BF16)

    def conv_out(p):
        st[p]["y_conv"] = jnp.dot(st[p].pop("yc"), w_co[...], preferred_element_type=F32)

    def merge(p):
        gates = _sigmoid(st[p].pop("gates"))
        merged = gates[:, :D_MODEL] * st[p].pop("y_conv") + gates[:, D_MODEL:] * st[p].pop("y_attn")
        st[p]["mix"] = jnp.dot(merged.astype(BF16), w_o[...], preferred_element_type=F32)

    def ffn(p):
        x1 = _layer_norm(DN_ALPHA * st[p].pop("x") + st[p].pop("mix"), ln1_g[...], ln1_b[...])
        x1b = x1.astype(BF16)
        ff = None
        for c in range(D_FF // FF_CHUNK):
            cs = slice(c * FF_CHUNK, (c + 1) * FF_CHUNK)
            h = jnp.maximum(jnp.dot(x1b, w_up[:, cs], preferred_element_type=F32), 0.0)
            d = jnp.dot((h * h).astype(BF16), w_dn[cs, :], preferred_element_type=F32)
            ff = d if ff is None else ff + d
        x2 = _layer_norm(DN_ALPHA * x1 + ff, ln2_g[...], ln2_b[...])
        y_ref[:, rows_of(p), :] = x2.reshape(nb, -1, D_MODEL)

    for p in range(n_parts):
        glu(p)
    for p in range(n_parts):
        gate_dots(p)
    conv(0)
    conv_out(0)
    merge(0)
    for p in range(1, n_parts):
        conv(p)
        ffn(p - 1)
        conv_out(p)
        merge(p)
    ffn(n_parts - 1)

    u_scr[:, 0:HIST_PAD, :] = u_scr[:, tm:tm + HIST_PAD, :]
    cs_ref[...] = u_scr[:, 0:HIST_PAD, :]


def _tail(x, o, hist, weights, nb, tm):
    bsz, t, _ = x.shape
    kern = functools.partial(_tail_kernel, nb=nb, tm=tm)
    tile = lambda b, i: (b, i, 0)

    def resident(w):
        return pl.BlockSpec(w.shape, lambda b, i: (0,) * w.ndim,
                            pipeline_mode=pl.Buffered(1))

    return pl.pallas_call(
        kern,
        grid=(bsz // nb, t // tm),
        in_specs=[
            pl.BlockSpec((nb, tm, D_MODEL), tile),
            pl.BlockSpec((nb, tm, D_ATTN), tile),
            pl.BlockSpec((nb, HIST_PAD, C_CONV), lambda b, i: (b, 0, 0)),
        ] + [resident(w) for w in weights],
        out_specs=[
            pl.BlockSpec((nb, tm, D_MODEL), tile),
            pl.BlockSpec((nb, HIST_PAD, C_CONV), lambda b, i: (b, 0, 0)),
        ],
        out_shape=[
            jax.ShapeDtypeStruct((bsz, t, D_MODEL), F32),
            jax.ShapeDtypeStruct((bsz, HIST_PAD, C_CONV), F32),
        ],
        scratch_shapes=[pltpu.VMEM((nb, HIST_PAD + tm, C_CONV), F32)],
        compiler_params=pltpu.CompilerParams(
            dimension_semantics=("arbitrary", "arbitrary"),
            vmem_limit_bytes=VMEM_LIMIT),
        name="tail",
    )(x, o, hist, *weights)


def _group(x, hist, past_kt, past_v, past_lft, w_qkvf, b_f, tail_w, *, tm_proj, tq, tk, nb, tm_tail):
    bsz, t, _ = x.shape
    q, k, v, kt, vb, lft = _proj(x, w_qkvf, b_f, tm_proj)
    lf = jnp.swapaxes(lft, 1, 2)
    if past_kt is not None:
        past = past_kt.shape[2]
        pad = -(-(past + t) // tk) * tk - past - t
        lft_all = jnp.pad(jnp.concatenate([past_lft, lft], axis=2), ((0, 0), (0, 0), (0, pad)))
        o = _attn(q, past_kt, past_v, lft_all, tq, tk, past, kt_new=kt, v_new=vb)
    else:
        o = _attn(q, kt, vb, lft, tq, tk, 0)
    hist = jnp.pad(hist, ((0, 0), (HIST_PAD - HIST, 0), (0, 0)))
    y, cs = _tail(x, o, hist, tail_w, nb, tm_tail)
    hd = (bsz, t, N_HEADS, HEAD_DIM)
    return y, cs[None, :, HIST_PAD - HIST:, :], k.reshape(hd)[None], v.reshape(hd)[None], lf[None]


def kernel(x_prompt, x_sample, cache_conv, cache_k, cache_v, cache_logf, w_in, b_f, w_dw, b_dw,
           ln_conv_g, ln_conv_b, w_conv_out, w_attn_out, w_o, ln1_g, ln1_b, w_up, w_down,
           ln2_g, ln2_b):
    assert w_in.shape[0] == 1, "single layer"
    w = w_in[0]
    row = lambda a: a[0][None, :].astype(F32)
    w_qkvf = jnp.concatenate(
        [w[:, OFF_Q:OFF_K] * (ATTN_SCALE * LOG2E), w[:, OFF_K:OFF_F],
         jnp.pad(w[:, OFF_F:OFF_G], ((0, 0), (0, F_PAD - N_HEADS)))], axis=1).astype(BF16)
    tail_w = (
        w[:, 0:OFF_Q].astype(BF16), w[:, OFF_G:].astype(BF16),
        w_dw[0].astype(F32), row(b_dw), row(ln_conv_g), row(ln_conv_b),
        w_conv_out[0].astype(BF16), w_attn_out[0].astype(BF16), w_o[0].astype(BF16),
        row(ln1_g), row(ln1_b), w_up[0].astype(BF16), w_down[0].astype(BF16),
        row(ln2_g), row(ln2_b))
    bf = jnp.pad(row(b_f), ((0, 0), (0, F_PAD - N_HEADS)))

    bp, tp, _ = x_prompt.shape
    bs, ts, _ = x_sample.shape
    yp, cp, kp, vp, lp = _group(
        x_prompt, jnp.zeros((bp, HIST, C_CONV), F32), None, None, None, w_qkvf, bf, tail_w,
        tm_proj=1024, tq=512, tk=512, nb=1, tm_tail=512)

    past = cache_k.shape[2]
    past_kt = jnp.swapaxes(cache_k[0].reshape(bs, past, D_ATTN), 1, 2).astype(BF16)
    past_v = cache_v[0].reshape(bs, past, D_ATTN).astype(BF16)
    past_lft = jnp.swapaxes(cache_logf[0].astype(F32), 1, 2)
    tk_s = -(-(past + ts) // LANES) * LANES
    ys, cs, ks, vs, ls = _group(
        x_sample, cache_conv[0].astype(F32), past_kt, past_v, past_lft, w_qkvf, bf, tail_w,
        tm_proj=ts, tq=ts, tk=tk_s, nb=bs, tm_tail=ts)
    return (yp, ys, cp, kp, vp, lp, cs, ks, vs, ls)
```
